```python
import math
import jax, jax.numpy as jnp
from jax import lax
import numpy as np

D_MODEL = 1024
BATCH = 4
SEQ = 8192
DEPTH = 2
DEC_BATCH = 32
DEC_SEQ = 8
PAST_LEN = 16384
PAGE_SIZE = 128

HEAD_DIM = 64
SSD_WIDTH = 3 * D_MODEL // 8
SSD_HEADS = SSD_WIDTH // HEAD_DIM
SSD_GROUPS = 2
SSD_STATE = 128
SSD_CONV = 4
SSD_CHUNK = 128
SSD_XBC = SSD_WIDTH + 2 * SSD_GROUPS * SSD_STATE
SC_WIDTH = D_MODEL // 4
SC_GROUPS = 4
SC_CONV = 3
ATT_WIDTH = 3 * D_MODEL // 8
ATT_HEADS = ATT_WIDTH // HEAD_DIM
ATT_KV_HEADS = 2
GQA = ATT_HEADS // ATT_KV_HEADS
KV_WIDTH = ATT_KV_HEADS * HEAD_DIM
CMP_LEN = 32
CMP_STRIDE = 16
SEL_BLOCK = 64
N_SEL = 16
WINDOW = 512
Q_BLOCK = 128
ROPE_THETA = 500000.0
ROT_DIM = HEAD_DIM // 4
MIX_WIDTH = SSD_WIDTH + SC_WIDTH + ATT_WIDTH
D_FF = 4 * D_MODEL
EPS = 1e-6
NEG = -1e30
FORCE = 1e4
IN_SIZES = (SSD_WIDTH, SSD_XBC, SSD_HEADS,
            SC_WIDTH, SC_WIDTH, SC_WIDTH,
            ATT_WIDTH,
            KV_WIDTH, KV_WIDTH, KV_WIDTH, KV_WIDTH, KV_WIDTH, KV_WIDTH,
            ATT_HEADS * 3)

kernel_name = "hymba_ssd_shortconv_nsa_decode_step"


def rmsnorm(x, w):
    xf = x.astype(jnp.float32)
    y = xf * lax.rsqrt(jnp.mean(xf * xf, axis=-1, keepdims=True) + EPS)
    return (y * w.astype(jnp.float32)).astype(x.dtype)


def masked_softmax(s, mask):
    s = jnp.where(mask, s.astype(jnp.float32), NEG)
    return jax.nn.softmax(s, axis=-1) * mask


def rope_partial(x, pos):
    half = ROT_DIM // 2
    inv_freq = ROPE_THETA ** (-jnp.arange(half, dtype=jnp.float32) * 2.0 / ROT_DIM)
    ang = pos.astype(jnp.float32)[:, None] * inv_freq
    cos = jnp.cos(ang)[None, :, None, :]
    sin = jnp.sin(ang)[None, :, None, :]
    xr = x[..., :ROT_DIM].astype(jnp.float32)
    x1, x2 = xr[..., :half], xr[..., half:]
    rot = jnp.concatenate([x1 * cos - x2 * sin, x2 * cos + x1 * sin], axis=-1)
    return jnp.concatenate([rot.astype(x.dtype), x[..., ROT_DIM:]], axis=-1)


def causal_dwconv(x, prev, w):
    width = w.shape[0]
    L = x.shape[1]
    xp = jnp.concatenate([prev.astype(x.dtype), x], axis=1)
    out = sum(xp[:, k:k + L] * w[k] for k in range(width))
    return out, xp[:, -(width - 1):]


def ssd_scan(x, dt, a, bm, cm, s0):
    dtype = x.dtype
    b, L, H, P = x.shape
    R = H // SSD_GROUPS
    cs = min(SSD_CHUNK, L)
    nc = -(-L // cs)
    pad = nc * cs - L
    def padl(t):
        return jnp.pad(t.astype(jnp.float32), [(0, 0), (0, pad)] + [(0, 0)] * (t.ndim - 2))
    x = padl(x).reshape(b, nc, cs, SSD_GROUPS, R, P)
    dt = padl(dt).reshape(b, nc, cs, SSD_GROUPS, R)
    bm = padl(bm).reshape(b, nc, cs, SSD_GROUPS, SSD_STATE)
    cm = padl(cm).reshape(b, nc, cs, SSD_GROUPS, SSD_STATE)
    acum = jnp.cumsum(dt * a.reshape(SSD_GROUPS, R), axis=2)
    seg = acum[:, :, :, None] - acum[:, :, None, :]
    causal = jnp.tril(jnp.ones((cs, cs), bool))[:, :, None, None]
    decay = jnp.where(causal, jnp.exp(jnp.where(causal, seg, 0.0)), 0.0)
    xdt = x * dt[..., None]
    cb = jnp.einsum('bcign,bcjgn->bcijg', cm, bm)
    y_diag = jnp.einsum('bcijg,bcijgr,bcjgrp->bcigrp', cb, decay, xdt)
    dec_end = jnp.exp(acum[:, :, -1:] - acum)
    states = jnp.einsum('bcjgn,bcjgr,bcjgrp->bcgrpn', bm, dec_end, xdt)
    chunk_dec = jnp.exp(acum[:, :, -1])

    def step(s, inp):
        st, cd = inp
        return s * cd[..., None, None] + st, s

    s_init = s0.astype(jnp.float32).reshape(b, SSD_GROUPS, R, P, SSD_STATE)
    s_fin, s_prev = lax.scan(step, s_init, (jnp.moveaxis(states, 1, 0), jnp.moveaxis(chunk_dec, 1, 0)))
    s_prev = jnp.moveaxis(s_prev, 0, 1)
    y_off = jnp.einsum('bcign,bcgrpn,bcigr->bcigrp', cm, s_prev, jnp.exp(acum))
    y = (y_diag + y_off).reshape(b, nc * cs, H, P)[:, :L]
    return y.astype(dtype), s_fin.reshape(b, H, P, SSD_STATE).astype(dtype)


def ssd_mixer(z, xbc, dt_raw, ssm0, conv0, conv_w, conv_b, dt_bias, a_log, d_skip, norm_w):
    b, L, _ = xbc.shape
    xbc_c, conv_new = causal_dwconv(xbc, conv0, conv_w)
    xbc_c = jax.nn.silu(xbc_c + conv_b)
    xs, bm, cm = jnp.split(xbc_c, [SSD_WIDTH, SSD_WIDTH + SSD_GROUPS * SSD_STATE], axis=-1)
    xs = xs.reshape(b, L, SSD_HEADS, HEAD_DIM)
    bm = bm.reshape(b, L, SSD_GROUPS, SSD_STATE)
    cm = cm.reshape(b, L, SSD_GROUPS, SSD_STATE)
    dt = jax.nn.softplus((dt_raw + dt_bias).astype(jnp.float32))
    a = -jnp.exp(a_log.astype(jnp.float32))
    y, ssm_new = ssd_scan(xs, dt, a, bm, cm, ssm0)
    y = (y + d_skip[:, None] * xs).reshape(b, L, SSD_WIDTH)
    y = rmsnorm(y * jax.nn.silu(z), norm_w)
    return y, ssm_new, conv_new


def short_conv_mixer(gate_b, gate_c, h_in, conv0, conv_w):
    y, conv_new = causal_dwconv(gate_c * h_in, conv0, conv_w)
    return gate_b * y, conv_new


def compress(k, pe, w1, w2):
    b, t = k.shape[:2]
    u = t // CMP_STRIDE
    units = k[:, :u * CMP_STRIDE].reshape(b, u, CMP_STRIDE, ATT_KV_HEADS, HEAD_DIM)
    w1h = w1.reshape(CMP_LEN // CMP_STRIDE, CMP_STRIDE, HEAD_DIM, HEAD_DIM)
    pe_term = jnp.einsum('rd,rde->e', pe, w1)
    first = jnp.einsum('burkd,rde->buke', units, w1h[0])[:, :-1]
    second = jnp.einsum('burkd,rde->buke', units, w1h[1])[:, 1:]
    return jax.nn.silu(first + second + pe_term) @ w2


def prompt_block_gather(k_sel, v_sel):
    b, L = k_sel.shape[:2]
    nb = -(-L // SEL_BLOCK)
    padn = nb * SEL_BLOCK - L
    def blockify(t):
        return jnp.pad(t, ((0, 0), (0, padn), (0, 0), (0, 0))).reshape(b, nb, SEL_BLOCK, ATT_KV_HEADS, HEAD_DIM)
    kb, vb = blockify(k_sel), blockify(v_sel)
    bi = jnp.arange(b)[:, None, None, None]
    hi = jnp.arange(ATT_KV_HEADS)[None, None, :, None]

    def gather(idx):
        j = jnp.clip(idx, 0, nb - 1)
        return kb[bi, j, :, hi], vb[bi, j, :, hi]
    return gather


def paged_block_gather(pool_k, pool_v, page_table, k_new, v_new):
    n_phys = pool_k.shape[0]
    bpp = PAGE_SIZE // SEL_BLOCK
    pk = pool_k.reshape(n_phys * bpp, SEL_BLOCK, ATT_KV_HEADS, HEAD_DIM)
    pv = pool_v.reshape(n_phys * bpp, SEL_BLOCK, ATT_KV_HEADS, HEAD_DIM)
    db, n_pages = page_table.shape
    nb_past = n_pages * bpp
    ln = k_new.shape[1]
    nb_new = -(-ln // SEL_BLOCK)
    padn = nb_new * SEL_BLOCK - ln
    def blockify(t):
        return jnp.pad(t, ((0, 0), (0, padn), (0, 0), (0, 0))).reshape(db, nb_new, SEL_BLOCK, ATT_KV_HEADS, HEAD_DIM)
    nk, nv = blockify(k_new), blockify(v_new)
    bi = jnp.arange(db)[:, None, None, None]
    hi = jnp.arange(ATT_KV_HEADS)[None, None, :, None]

    def gather(idx):
        jp = jnp.clip(idx, 0, nb_past - 1)
        phys = page_table[bi, jp // bpp] * bpp + jp % bpp
        jn = jnp.clip(idx - nb_past, 0, nb_new - 1)
        in_past = (idx < nb_past)[..., None, None]
        k = jnp.where(in_past, pk[phys, :, hi], nk[bi, jn, :, hi])
        v = jnp.where(in_past, pv[phys, :, hi], nv[bi, jn, :, hi])
        return k, v
    return gather


def nsa_attend(q_rope, q_raw, gates, cmp_k, cmp_v, gather_sel, n_blocks, kw, vw, k_off, q_off):
    b, lq = q_rope.shape[:2]
    qb = min(Q_BLOCK, lq)
    nqb = -(-lq // qb)
    lpad = nqb * qb - lq

    def blocks(a):
        a = jnp.pad(a, [(0, 0), (0, lpad)] + [(0, 0)] * (a.ndim - 2))
        a = a.reshape((b, nqb, qb, ATT_KV_HEADS, GQA) + a.shape[3:])
        return jnp.moveaxis(a, 1, 0)

    kw_pad = jnp.pad(kw, ((0, 0), (WINDOW, lpad), (0, 0), (0, 0)))
    vw_pad = jnp.pad(vw, ((0, 0), (WINDOW, lpad), (0, 0), (0, 0)))
    n_cmp = cmp_k.shape[1]
    cmp_end = jnp.arange(n_cmp) * CMP_STRIDE + CMP_LEN - 1
    nbp = max(n_blocks, N_SEL)
    units_per_blk = SEL_BLOCK // CMP_STRIDE
    blk_ids = jnp.arange(nbp)
    scale = HEAD_DIM ** -0.5

    def one_block(args):
        qr, qn, g, i = args
        t0 = q_off + i * qb
        tq = t0 + jnp.arange(qb)
        s_c = jnp.einsum('bqkgd,bnkd->bqkgn', qn, cmp_k) * scale
        m_c = (cmp_end[None, :] <= tq[:, None])[None, :, None, None, :]
        p_c = masked_softmax(s_c, m_c)
        o_c = jnp.einsum('bqkgn,bnkd->bqkgd', p_c.astype(cmp_v.dtype), cmp_v)
        imp = p_c.sum(axis=3)
        imp = jnp.pad(imp, ((0, 0), (0, 0), (0, 0), (0, 1))) + jnp.pad(imp, ((0, 0), (0, 0), (0, 0), (1, 0)))
        imp = jnp.pad(imp, ((0, 0), (0, 0), (0, 0), (0, nbp * units_per_blk - (n_cmp + 1))))
        imp = imp.reshape(imp.shape[:3] + (nbp, units_per_blk)).sum(-1)
        cur = tq // SEL_BLOCK
        causal = blk_ids[None, :] * SEL_BLOCK <= tq[:, None]
        forced = causal & ((blk_ids[None, :] == 0) | (blk_ids[None, :] == cur[:, None]) | (blk_ids[None, :] == cur[:, None] - 1))
        score = jnp.where(forced[None, :, None, :], FORCE, jnp.where(causal[None, :, None, :], imp, -FORCE))
        top_v, top_i = lax.top_k(score, N_SEL)
        ks, vs = gather_sel(top_i)
        kpos = top_i[..., None] * SEL_BLOCK + jnp.arange(SEL_BLOCK)
        m_s = (top_v > -FORCE / 2)[..., None] & (kpos <= tq[None, :, None, None, None])
        s_s = jnp.einsum('bqkgd,bqknrd->bqkgnr', qr, ks) * scale
        s_s = s_s.reshape(s_s.shape[:4] + (N_SEL * SEL_BLOCK,))
        m_s = m_s.reshape(m_s.shape[:3] + (1, N_SEL * SEL_BLOCK))
        p_s = masked_softmax(s_s, m_s)
        o_s = jnp.einsum('bqkgm,bqkmd->bqkgd', p_s.astype(vs.dtype), vs.reshape(vs.shape[:3] + (N_SEL * SEL_BLOCK, HEAD_DIM)))
        kwb = lax.dynamic_slice_in_dim(kw_pad, t0 - k_off, WINDOW + qb, axis=1)
        vwb = lax.dynamic_slice_in_dim(vw_pad, t0 - k_off, WINDOW + qb, axis=1)
        spos = t0 - WINDOW + jnp.arange(WINDOW + qb)
        m_w = (spos[None, :] <= tq[:, None]) & (spos[None, :] > tq[:, None] - WINDOW) & (spos[None, :] >= k_off)
        s_w = jnp.einsum('bqkgd,bskd->bqkgs', qr, kwb) * scale
        p_w = masked_softmax(s_w, m_w[None, :, None, None, :])
        o_w = jnp.einsum('bqkgs,bskd->bqkgd', p_w.astype(vwb.dtype), vwb)
        return g[..., 0:1] * o_c + g[..., 1:2] * o_s + g[..., 2:3] * o_w

    out = lax.map(one_block, (blocks(q_rope), blocks(q_raw), blocks(gates), jnp.arange(nqb, dtype=jnp.int32)))
    out = jnp.moveaxis(out, 0, 1).reshape(b, nqb * qb, ATT_HEADS, HEAD_DIM)
    return out[:, :lq]


def run_layer(h, pos, ssm0, ssd_conv0, sc_conv0, attn_fn,
              norm1_w, w_in, ssd_conv_w, ssd_conv_b, ssd_dt_bias, ssd_a_log, ssd_d, ssd_norm_w,
              sc_conv_w, w_out, norm2_w, w_ff1, w_ff2):
    b, L, _ = h.shape
    xn = rmsnorm(h, norm1_w)
    offsets = [int(v) for v in np.cumsum(IN_SIZES)[:-1]]
    (z, xbc, dt_raw, sc_b, sc_c, sc_h, q, k_cmp, v_cmp, k_sel, v_sel, k_win, v_win, gates) = jnp.split(xn @ w_in, offsets, axis=-1)
    y_ssd, ssm_new, ssd_conv_new = ssd_mixer(z, xbc, dt_raw, ssm0, ssd_conv0, ssd_conv_w, ssd_conv_b,
                                             ssd_dt_bias, ssd_a_log, ssd_d, ssd_norm_w)
    y_sc, sc_conv_new = short_conv_mixer(sc_b, sc_c, sc_h, sc_conv0, sc_conv_w)
    def heads(a, n):
        return a.reshape(b, L, n, HEAD_DIM)
    q = heads(q, ATT_HEADS)
    k_cmp, v_cmp = heads(k_cmp, ATT_KV_HEADS), heads(v_cmp, ATT_KV_HEADS)
    k_sel, v_sel = rope_partial(heads(k_sel, ATT_KV_HEADS), pos), heads(v_sel, ATT_KV_HEADS)
    k_win, v_win = rope_partial(heads(k_win, ATT_KV_HEADS), pos), heads(v_win, ATT_KV_HEADS)
    g = jax.nn.sigmoid(gates.reshape(b, L, ATT_HEADS, 3))
    y_att = attn_fn(rope_partial(q, pos), q, g, k_cmp, v_cmp, k_sel, v_sel, k_win, v_win).reshape(b, L, ATT_WIDTH)
    h = h + jnp.concatenate([y_ssd, y_sc, y_att], axis=-1) @ w_out
    hn = rmsnorm(h, norm2_w)
    h = h + jnp.square(jax.nn.relu(hn @ w_ff1)) @ w_ff2
    return h, ssm_new, ssd_conv_new, sc_conv_new, (k_cmp, v_cmp, k_sel, v_sel, k_win, v_win)


def setup_inputs(seed: int = 0) -> dict:
    key = jax.random.key(seed)
    k = jax.random.split(key, 30)
    f32 = jnp.float32
    n_pages = PAST_LEN // PAGE_SIZE
    n_phys = (DEC_BATCH * n_pages * 5) // 4
    win_buf = min(WINDOW, PAST_LEN)
    in_width = sum(IN_SIZES)
    def nrm(kk, shape, scale):
        return jax.random.normal(kk, shape, f32) * scale
    page_shape = (DEPTH, n_phys, PAGE_SIZE, ATT_KV_HEADS, HEAD_DIM)
    win_shape = (DEPTH, DEC_BATCH, win_buf, ATT_KV_HEADS, HEAD_DIM)
    page_table = jax.random.permutation(k[11], n_phys)[:DEC_BATCH * n_pages].reshape(DEC_BATCH, n_pages).astype(jnp.int32)
    dt0 = jnp.exp(jax.random.uniform(k[16], (DEPTH, SSD_HEADS), f32, math.log(1e-3), math.log(1e-1)))
    return {
        'x_prompt': nrm(k[0], (BATCH, SEQ, D_MODEL), 1.0),
        'x_sample': nrm(k[1], (DEC_BATCH, DEC_SEQ, D_MODEL), 1.0),
        'cache_k_cmp': nrm(k[2], page_shape, 1.0),
        'cache_v_cmp': nrm(k[3], page_shape, 1.0),
        'cache_k_sel': nrm(k[4], page_shape, 1.0),
        'cache_v_sel': nrm(k[5], page_shape, 1.0),
        'cache_k_win': nrm(k[6], win_shape, 1.0),
        'cache_v_win': nrm(k[7], win_shape, 1.0),
        'state_ssm': nrm(k[8], (DEPTH, DEC_BATCH, SSD_HEADS, HEAD_DIM, SSD_STATE), 0.1),
        'state_ssd_conv': nrm(k[9], (DEPTH, DEC_BATCH, SSD_CONV - 1, SSD_XBC), 1.0),
        'state_sc_conv': nrm(k[10], (DEPTH, DEC_BATCH, SC_CONV - 1, SC_WIDTH), 1.0),
        'page_table': page_table,
        'norm1_w': 1.0 + nrm(k[12], (DEPTH, D_MODEL), 0.02),
        'w_in': nrm(k[13], (DEPTH, D_MODEL, in_width), D_MODEL ** -0.5),
        'ssd_conv_w': nrm(k[14], (DEPTH, SSD_CONV, SSD_XBC), SSD_CONV ** -0.5),
        'ssd_conv_b': nrm(k[15], (DEPTH, SSD_XBC), 0.02),
        'ssd_dt_bias': dt0 + jnp.log(-jnp.expm1(-dt0)),
        'ssd_a_log': jnp.log(jax.random.uniform(k[17], (DEPTH, SSD_HEADS), f32, 1.0, 16.0)),
        'ssd_d': 1.0 + nrm(k[18], (DEPTH, SSD_HEADS), 0.1),
        'ssd_norm_w': 1.0 + nrm(k[19], (DEPTH, SSD_WIDTH), 0.02),
        'sc_conv_w': nrm(k[20], (DEPTH, SC_CONV, SC_WIDTH), SC_CONV ** -0.5),
        'cmp_pe': nrm(k[21], (DEPTH, 2, CMP_LEN, HEAD_DIM), 0.1),
        'cmp_w1': nrm(k[22], (DEPTH, 2, CMP_LEN, HEAD_DIM, HEAD_DIM), (CMP_LEN * HEAD_DIM) ** -0.5),
        'cmp_w2': nrm(k[23], (DEPTH, 2, HEAD_DIM, HEAD_DIM), HEAD_DIM ** -0.5),
        'w_out': nrm(k[24], (DEPTH, MIX_WIDTH, D_MODEL), MIX_WIDTH ** -0.5),
        'norm2_w': 1.0 + nrm(k[25], (DEPTH, D_MODEL), 0.02),
        'w_ff1': nrm(k[26], (DEPTH, D_MODEL, D_FF), D_MODEL ** -0.5),
        'w_ff2': nrm(k[27], (DEPTH, D_FF, D_MODEL), D_FF ** -0.5),
        'final_norm_w': 1.0 + nrm(k[28], (D_MODEL,), 0.02),
    }


def reference(x_prompt, x_sample, cache_k_cmp, cache_v_cmp, cache_k_sel, cache_v_sel, cache_k_win, cache_v_win,
              state_ssm, state_ssd_conv, state_sc_conv, page_table,
              norm1_w, w_in, ssd_conv_w, ssd_conv_b, ssd_dt_bias, ssd_a_log, ssd_d, ssd_norm_w, sc_conv_w,
              cmp_pe, cmp_w1, cmp_w2, w_out, norm2_w, w_ff1, w_ff2, final_norm_w):
    bp, lp = x_prompt.shape[:2]
    db, ls = x_sample.shape[:2]
    n_pages = page_table.shape[1]
    past = n_pages * PAGE_SIZE
    win_buf = cache_k_win.shape[2]
    pos_p = jnp.arange(lp, dtype=jnp.int32)
    pos_s = past + jnp.arange(ls, dtype=jnp.int32)
    dty = x_prompt.dtype
    ssm0_p = jnp.zeros((bp, SSD_HEADS, HEAD_DIM, SSD_STATE), dty)
    ssd_conv0_p = jnp.zeros((bp, SSD_CONV - 1, SSD_XBC), dty)
    sc_conv0_p = jnp.zeros((bp, SC_CONV - 1, SC_WIDTH), dty)
    hp, hs = x_prompt, x_sample
    p_lists = [[] for _ in range(9)]
    s_lists = [[] for _ in range(9)]
    for l in range(DEPTH):
        layer_w = (norm1_w[l], w_in[l], ssd_conv_w[l], ssd_conv_b[l], ssd_dt_bias[l], ssd_a_log[l], ssd_d[l],
                   ssd_norm_w[l], sc_conv_w[l], w_out[l], norm2_w[l], w_ff1[l], w_ff2[l])

        def prompt_attn(qr, qn, g, kc, vc, ks, vs, kw, vw, l=l):
            ckc = compress(kc, cmp_pe[l, 0], cmp_w1[l, 0], cmp_w2[l, 0])
            cvc = compress(vc, cmp_pe[l, 1], cmp_w1[l, 1], cmp_w2[l, 1])
            return nsa_attend(qr, qn, g, ckc, cvc, prompt_block_gather(ks, vs), -(-lp // SEL_BLOCK), kw, vw, 0, 0)

        def sample_attn(qr, qn, g, kc, vc, ks, vs, kw, vw, l=l):
            past_kc = cache_k_cmp[l][page_table].reshape(db, past, ATT_KV_HEADS, HEAD_DIM)
            past_vc = cache_v_cmp[l][page_table].reshape(db, past, ATT_KV_HEADS, HEAD_DIM)
            ckc = compress(jnp.concatenate([past_kc, kc], axis=1), cmp_pe[l, 0], cmp_w1[l, 0], cmp_w2[l, 0])
            cvc = compress(jnp.concatenate([past_vc, vc], axis=1), cmp_pe[l, 1], cmp_w1[l, 1], cmp_w2[l, 1])
            gather = paged_block_gather(cache_k_sel[l], cache_v_sel[l], page_table, ks, vs)
            kw_all = jnp.concatenate([cache_k_win[l], kw], axis=1)
            vw_all = jnp.concatenate([cache_v_win[l], vw], axis=1)
            return nsa_attend(qr, qn, g, ckc, cvc, gather, -(-(past + ls) // SEL_BLOCK), kw_all, vw_all, past - win_buf, past)

        hp, ssm_p, sdc_p, scc_p, rows_p = run_layer(hp, pos_p, ssm0_p, ssd_conv0_p, sc_conv0_p, prompt_attn, *layer_w)
        hs, ssm_s, sdc_s, scc_s, rows_s = run_layer(hs, pos_s, state_ssm[l], state_ssd_conv[l], state_sc_conv[l], sample_attn, *layer_w)
        keep_p = min(WINDOW, lp)
        keep_s = min(WINDOW, win_buf + ls)
        p_new = (rows_p[0], rows_p[1], rows_p[2], rows_p[3], rows_p[4][:, -keep_p:], rows_p[5][:, -keep_p:], ssm_p, sdc_p, scc_p)
        s_new = (rows_s[0], rows_s[1], rows_s[2], rows_s[3],
                 jnp.concatenate([cache_k_win[l], rows_s[4]], axis=1)[:, -keep_s:],
                 jnp.concatenate([cache_v_win[l], rows_s[5]], axis=1)[:, -keep_s:], ssm_s, sdc_s, scc_s)
        for lst, arr in zip(p_lists, p_new):
            lst.append(arr)
        for lst, arr in zip(s_lists, s_new):
            lst.append(arr)
    (p_k_cmp, p_v_cmp, p_k_sel, p_v_sel, p_k_win, p_v_win, p_ssm, p_ssd_conv, p_sc_conv) = [jnp.stack(a) for a in p_lists]
    (s_k_cmp, s_v_cmp, s_k_sel, s_v_sel, s_k_win, s_v_win, s_ssm, s_ssd_conv, s_sc_conv) = [jnp.stack(a) for a in s_lists]
    y_prompt = rmsnorm(hp, final_norm_w)
    y_sample = rmsnorm(hs, final_norm_w)
    return (y_prompt, y_sample,
            p_k_cmp, p_v_cmp, p_k_sel, p_v_sel, p_k_win, p_v_win, p_ssm, p_ssd_conv, p_sc_conv,
            s_k_cmp, s_v_cmp, s_k_sel, s_v_sel, s_k_win, s_v_win, s_ssm, s_ssd_conv, s_sc_conv)
```

```python
import functools
import math

import jax
import jax.numpy as jnp
from jax import lax
from jax.experimental import pallas as pl
from jax.experimental.pallas import tpu as pltpu

F32 = jnp.float32
BF16 = jnp.bfloat16
HIGHEST = lax.Precision.HIGHEST

D_MODEL = 1024
HEAD_DIM = 64
SSD_WIDTH = 384
SSD_HEADS = 6
SSD_GROUPS = 2
SSD_STATE = 128
SSD_CONV = 4
SSD_CHUNK = 128
SSD_XBC = SSD_WIDTH + 2 * SSD_GROUPS * SSD_STATE
SC_WIDTH = 256
SC_CONV = 3
ATT_WIDTH = 384
ATT_HEADS = 6
ATT_KV_HEADS = 2
GQA = ATT_HEADS // ATT_KV_HEADS
KV_WIDTH = ATT_KV_HEADS * HEAD_DIM
CMP_LEN = 32
CMP_STRIDE = 16
SEL_BLOCK = 64
N_SEL = 16
WINDOW = 512
Q_BLOCK = 128
PAGE_SIZE = 128
ROPE_THETA = 500000.0
ROT_DIM = HEAD_DIM // 4
D_FF = 4 * D_MODEL
EPS = 1e-6
NEG = -1e30
FORCE = 1e4
IN_SIZES = (SSD_WIDTH, SSD_XBC, SSD_HEADS, SC_WIDTH, SC_WIDTH, SC_WIDTH, ATT_WIDTH,
            KV_WIDTH, KV_WIDTH, KV_WIDTH, KV_WIDTH, KV_WIDTH, KV_WIDTH, ATT_HEADS * 3)

LANES = 128
SUBLANES = 8
VMEM_LIMIT_BYTES = 56 * 1024 * 1024

COL_Z = 0
COL_XBC = COL_Z + SSD_WIDTH
COL_SC = COL_XBC + SSD_XBC
COL_Q = COL_SC + 3 * SC_WIDTH
COL_KV = COL_Q + ATT_WIDTH
COL_SMALL = COL_KV + 6 * KV_WIDTH
IN_PAD = COL_SMALL + LANES
GATE_LANE0 = SSD_HEADS
KEYS_PER_TILE = 512


def _dot(a, b):
    return jnp.dot(a, b, preferred_element_type=F32)


def _dot_nt(a, b):
    return lax.dot_general(a, b, (((1,), (1,)), ((), ())), preferred_element_type=F32)


def _dot_exact(a, b):
    return jnp.dot(a, b, preferred_element_type=F32, precision=HIGHEST)


def _silu(x):
    return x * (1.0 / (1.0 + jnp.exp(-x)))


def _sigmoid(x):
    return 1.0 / (1.0 + jnp.exp(-x))


def _rms(x, w):
    return x * lax.rsqrt(jnp.mean(x * x, axis=-1, keepdims=True) + EPS) * w


def _iota(shape, dim):
    return lax.broadcasted_iota(jnp.int32, shape, dim)


def _rope_chunk(x, cos_t, sin_t, ll):
    fwd = pltpu.roll(x, LANES - ROT_DIM // 2, 1)
    bwd = pltpu.roll(x, ROT_DIM // 2, 1)
    partner = jnp.where(ll < ROT_DIM // 2, fwd, bwd)
    return x * cos_t + partner * sin_t


def _inproj_kernel(x_ref, nw_ref, w_ref, invf_ref,
                   z_ref, xbc_ref, sc_ref, qraw_ref, qrope_ref,
                   kcmp_ref, vcmp_ref, ksel_ref, vsel_ref, kwin_ref, vwin_ref, small_ref,
                   kselb_ref, vselb_ref, kwinb_ref, vwinb_ref, *, tm, seq_len, pos0):
    i = pl.program_id(0)
    xn = _rms(x_ref[...], nw_ref[...]).astype(BF16)

    def proj(c0, width):
        return _dot(xn, w_ref[:, c0:c0 + width])

    z_ref[...] = proj(COL_Z, SSD_WIDTH)
    xbc_ref[...] = proj(COL_XBC, SSD_XBC)
    sc_ref[...] = proj(COL_SC, 3 * SC_WIDTH)

    row = i * tm + _iota((tm, LANES), 0)
    pos = (pos0 + (row & (seq_len - 1))).astype(F32)
    lane = _iota((tm, LANES), 1)
    ll = lane & (HEAD_DIM - 1)
    ang = pos * invf_ref[...]
    cos_a = jnp.cos(ang)
    sin_a = jnp.sin(ang)
    half = ROT_DIM // 2
    cos_t = jnp.where(ll < ROT_DIM, cos_a, 1.0)
    sin_t = jnp.where(ll < half, -sin_a, jnp.where(ll < ROT_DIM, sin_a, 0.0))

    q = proj(COL_Q, ATT_WIDTH)
    qraw_ref[...] = q
    qrope_ref[...] = jnp.concatenate(
        [_rope_chunk(q[:, c * LANES:(c + 1) * LANES], cos_t, sin_t, ll) for c in range(ATT_WIDTH // LANES)], axis=1)

    kv = proj(COL_KV, 6 * KV_WIDTH)
    kcmp_ref[...] = kv[:, 0:LANES]
    vcmp_ref[...] = kv[:, LANES:2 * LANES]
    ksel = _rope_chunk(kv[:, 2 * LANES:3 * LANES], cos_t, sin_t, ll)
    vsel = kv[:, 3 * LANES:4 * LANES]
    kwin = _rope_chunk(kv[:, 4 * LANES:5 * LANES], cos_t, sin_t, ll)
    vwin = kv[:, 5 * LANES:6 * LANES]
    ksel_ref[...] = ksel
    vsel_ref[...] = vsel
    kwin_ref[...] = kwin
    vwin_ref[...] = vwin
    kselb_ref[...] = ksel.astype(BF16)
    vselb_ref[...] = vsel.astype(BF16)
    kwinb_ref[...] = kwin.astype(BF16)
    vwinb_ref[...] = vwin.astype(BF16)

    sm = proj(COL_SMALL, LANES)
    is_gate = (lane >= GATE_LANE0) & (lane < GATE_LANE0 + 3 * ATT_HEADS)
    small_ref[...] = jnp.where(is_gate, _sigmoid(sm), sm)


def _inproj(h, norm_w, w_perm, invf, *, seq_len, pos0):
    t = h.shape[0]
    tm = min(256, t)
    assert t % tm == 0 and seq_len & (seq_len - 1) == 0
    widths = [SSD_WIDTH, SSD_XBC, 3 * SC_WIDTH, ATT_WIDTH, ATT_WIDTH] + [KV_WIDTH] * 6 + [LANES]
    dtypes = [F32] * len(widths) + [BF16] * 4
    widths = widths + [KV_WIDTH] * 4
    row_spec = lambda w: pl.BlockSpec((tm, w), lambda i: (i, 0))
    const_spec = lambda a: pl.BlockSpec(a.shape, lambda i: (0, 0))
    return pl.pallas_call(
        functools.partial(_inproj_kernel, tm=tm, seq_len=seq_len, pos0=pos0),
        grid=(t // tm,),
        in_specs=[row_spec(D_MODEL), const_spec(norm_w), const_spec(w_perm), const_spec(invf)],
        out_specs=[row_spec(w) for w in widths],
        out_shape=[jax.ShapeDtypeStruct((t, w), dt) for w, dt in zip(widths, dtypes)],
        compiler_params=pltpu.CompilerParams(dimension_semantics=("arbitrary",), vmem_limit_bytes=VMEM_LIMIT_BYTES),
        name="inproj",
    )(h, norm_w, w_perm, invf)


def _pad_rows(x, rows):
    if x.shape[0] == rows:
        return x
    return jnp.concatenate([x, jnp.zeros((rows - x.shape[0], x.shape[1]), x.dtype)], axis=0)


def _mixer_kernel(z_ref, xbc_ref, sc_ref, small_ref, ssm0_ref, cssd0_ref, csc0_ref,
                  cw_ref, cb_ref, dtb_ref, alog_ref, dskip_ref, nw_ref, scw_ref, tril_ref, expand_ref,
                  yssd_ref, ysc_ref, ssm_out_ref, cssd_out_ref, csc_out_ref,
                  s_scr, xp_scr, scp_scr, *, cs, n_chunks):
    C = SSD_CHUNK
    c = pl.program_id(1)

    @pl.when(c == 0)
    def _():
        s_scr[...] = ssm0_ref[0]
        xp_scr[0:SUBLANES, :] = cssd0_ref[0]
        scp_scr[0:SUBLANES, :] = csc0_ref[0]

    xp_scr[SUBLANES:SUBLANES + C, :] = _pad_rows(xbc_ref[...], C)
    cw = cw_ref[...]
    conv = cb_ref[...]
    for k in range(SSD_CONV):
        off = SUBLANES - (SSD_CONV - 1) + k
        conv = conv + cw[k:k + 1, :] * xp_scr[off:off + C, :]
    xbc_c = _silu(conv)
    tail_ssd = xp_scr[cs:cs + SUBLANES, :]
    xp_scr[0:SUBLANES, :] = tail_ssd

    xs = xbc_c[:, 0:SSD_WIDTH]
    bm = xbc_c[:, SSD_WIDTH:SSD_WIDTH + SSD_GROUPS * SSD_STATE]
    cm = xbc_c[:, SSD_WIDTH + SSD_GROUPS * SSD_STATE:SSD_XBC]

    lane = _iota((C, LANES), 1)
    rowi = _iota((C, LANES), 0)
    small = _pad_rows(small_ref[...], C)
    dt_raw = small + dtb_ref[...]
    dt = jnp.maximum(dt_raw, 0.0) + jnp.log1p(jnp.exp(-jnp.abs(dt_raw)))
    dt = jnp.where((lane < SSD_HEADS) & (rowi < cs), dt, 0.0)
    a_row = jnp.where(lane[0:1, :] < SSD_HEADS, -jnp.exp(alog_ref[...]), 0.0)
    acum = _dot_exact(tril_ref[...], dt * a_row)
    acum_t = acum.T
    a_last = acum[C - 1:C, :]
    expand = expand_ref[...]
    dt_w = _dot_exact(dt, expand)
    eacc_w = _dot_exact(jnp.exp(acum), expand)
    dend_w = _dot_exact(jnp.exp(a_last - acum), expand)
    cdec = jnp.exp(a_last)
    xdt = xs * dt_w
    xdtw_t = (xdt * dend_w).T

    causal = _iota((C, C), 0) >= _iota((C, C), 1)
    head_lo = lane < HEAD_DIM
    zeros_half = jnp.zeros((HEAD_DIM, SSD_STATE), BF16)
    y_chunks = [None] * (SSD_WIDTH // LANES)
    for g in range(SSD_GROUPS):
        bm_g = bm[:, g * SSD_STATE:(g + 1) * SSD_STATE].astype(BF16)
        cm_g = cm[:, g * SSD_STATE:(g + 1) * SSD_STATE].astype(BF16)
        cb = _dot_nt(cm_g, bm_g)
        for r in range(SSD_HEADS // SSD_GROUPS):
            h = g * (SSD_HEADS // SSD_GROUPS) + r
            ch, lo = h // 2, h % 2 == 0
            keep = head_lo if lo else jnp.logical_not(head_lo)
            seg = acum[:, h:h + 1] - acum_t[h:h + 1, :]
            decay = jnp.where(causal, jnp.exp(jnp.where(causal, seg, 0.0)), 0.0)
            xdt_h = jnp.where(keep, xdt[:, ch * LANES:(ch + 1) * LANES], 0.0).astype(BF16)
            y_d = _dot((cb * decay).astype(BF16), xdt_h)
            s_h = s_scr[h]
            s_b = s_h.astype(BF16)
            s_pl = jnp.concatenate([s_b, zeros_half] if lo else [zeros_half, s_b], axis=0)
            y_o = _dot_nt(cm_g, s_pl)
            y_h = y_d + eacc_w[:, ch * LANES:(ch + 1) * LANES] * y_o
            y_chunks[ch] = y_h if y_chunks[ch] is None else y_chunks[ch] + y_h
            s_scr[h] = s_h * cdec[:, h:h + 1] + _dot(xdtw_t[h * HEAD_DIM:(h + 1) * HEAD_DIM, :].astype(BF16), bm_g)
    y = jnp.concatenate(y_chunks, axis=1) + dskip_ref[...] * xs
    gated = y * _silu(_pad_rows(z_ref[...], C))
    yssd_ref[...] = _rms(gated, nw_ref[...])[0:cs]

    sc = _pad_rows(sc_ref[...], C)
    scp_scr[SUBLANES:SUBLANES + C, :] = sc[:, SC_WIDTH:2 * SC_WIDTH] * sc[:, 2 * SC_WIDTH:3 * SC_WIDTH]
    scw = scw_ref[...]
    conv3 = jnp.zeros((C, SC_WIDTH), F32)
    for k in range(SC_CONV):
        off = SUBLANES - (SC_CONV - 1) + k
        conv3 = conv3 + scw[k:k + 1, :] * scp_scr[off:off + C, :]
    ysc_ref[...] = (sc[:, 0:SC_WIDTH] * conv3)[0:cs]
    tail_sc = scp_scr[cs:cs + SUBLANES, :]
    scp_scr[0:SUBLANES, :] = tail_sc

    @pl.when(c == n_chunks - 1)
    def _():
        ssm_out_ref[0] = s_scr[...]
        cssd_out_ref[0] = tail_ssd
        csc_out_ref[0] = tail_sc


def _mixer(z, xbc, sc, small, ssm0, cssd0, csc0, lw, *, batch, seq_len):
    cs = min(SSD_CHUNK, seq_len)
    assert seq_len % cs == 0 and cs % SUBLANES == 0
    nc = seq_len // cs
    t = batch * seq_len
    row_spec = lambda w: pl.BlockSpec((cs, w), lambda b, c: (b * nc + c, 0))
    const_spec = lambda a: pl.BlockSpec(a.shape, lambda b, c: (0,) * a.ndim)
    batch_spec = lambda a: pl.BlockSpec((1,) + a.shape[1:], lambda b, c: (b,) + (0,) * (a.ndim - 1))
    consts = [lw["ssd_conv_w"], lw["ssd_conv_b"], lw["dt_bias"], lw["a_log"], lw["d_skip"], lw["ssd_norm_w"],
              lw["sc_conv_w"], lw["tril"], lw["expand"]]
    out_shape = [jax.ShapeDtypeStruct((t, SSD_WIDTH), F32), jax.ShapeDtypeStruct((t, SC_WIDTH), F32),
                 jax.ShapeDtypeStruct(ssm0.shape, F32), jax.ShapeDtypeStruct(cssd0.shape, F32),
                 jax.ShapeDtypeStruct(csc0.shape, F32)]
    return pl.pallas_call(
        functools.partial(_mixer_kernel, cs=cs, n_chunks=nc),
        grid=(batch, nc),
        in_specs=[row_spec(SSD_WIDTH), row_spec(SSD_XBC), row_spec(3 * SC_WIDTH), row_spec(LANES),
                  batch_spec(ssm0), batch_spec(cssd0), batch_spec(csc0)] + [const_spec(a) for a in consts],
        out_specs=[row_spec(SSD_WIDTH), row_spec(SC_WIDTH), batch_spec(ssm0), batch_spec(cssd0), batch_spec(csc0)],
        out_shape=out_shape,
        scratch_shapes=[pltpu.VMEM((SSD_HEADS, HEAD_DIM, SSD_STATE), F32),
                        pltpu.VMEM((SUBLANES + SSD_CHUNK, SSD_XBC), F32),
                        pltpu.VMEM((SUBLANES + SSD_CHUNK, SC_WIDTH), F32)],
        compiler_params=pltpu.CompilerParams(dimension_semantics=("arbitrary", "arbitrary"),
                                             vmem_limit_bytes=VMEM_LIMIT_BYTES),
        name="mixer",
    )(z, xbc, sc, small, ssm0, cssd0, csc0, *consts)


def _compress_units(units, pe0, pe1, w_first, w_second, w2):
    n_units = units.shape[0]
    first = _dot((units + pe0).astype(BF16), w_first)
    second = _dot((units + pe1).astype(BF16), w_second)
    pre = first + pltpu.roll(second, n_units - 1, 0)
    return _dot(_silu(pre).astype(BF16), w2)


def _compress_prompt_kernel(kc_ref, vc_ref, pe_ref, wf_ref, ws_ref, w2_ref, ck_ref, cv_ref):
    for idx, (src, dst) in enumerate(((kc_ref, ck_ref), (vc_ref, cv_ref))):
        dst[0] = _compress_units(src[0], pe_ref[idx, 0:1, :], pe_ref[idx, 1:2, :],
                                 wf_ref[idx], ws_ref[idx], w2_ref[idx]).astype(BF16)


def _compress_prompt(kc, vc, lw, *, batch, seq_len):
    n_units = seq_len // CMP_STRIDE
    unit_w = CMP_STRIDE * KV_WIDTH
    kc = kc.reshape(batch, n_units, unit_w)
    vc = vc.reshape(batch, n_units, unit_w)
    consts = [lw["cmp_pe"], lw["cmp_wf"], lw["cmp_ws"], lw["cmp_w2"]]
    const_spec = lambda a: pl.BlockSpec(a.shape, lambda b: (0,) * a.ndim)
    in_spec = pl.BlockSpec((1, n_units, unit_w), lambda b: (b, 0, 0))
    out_spec = pl.BlockSpec((1, n_units, KV_WIDTH), lambda b: (b, 0, 0))
    return pl.pallas_call(
        _compress_prompt_kernel,
        grid=(batch,),
        in_specs=[in_spec, in_spec] + [const_spec(a) for a in consts],
        out_specs=[out_spec, out_spec],
        out_shape=[jax.ShapeDtypeStruct((batch, n_units, KV_WIDTH), BF16)] * 2,
        compiler_params=pltpu.CompilerParams(dimension_semantics=("arbitrary",), vmem_limit_bytes=VMEM_LIMIT_BYTES),
        name="compress_prompt",
    )(kc, vc, *consts)


def _page_copy(cache_hbm, page, buf, slot, rows, sem):
    return pltpu.make_async_copy(cache_hbm.at[page], buf.at[pl.ds(slot * rows, rows)], sem)


def _gather_pages(pt_ref, b, cache_hbm, buf, sem, n_pages, rows):
    def start(p, carry):
        _page_copy(cache_hbm, pt_ref[b, p], buf, p, rows, sem).start()
        return carry
    lax.fori_loop(0, n_pages, start, 0)


def _wait_pages(cache_hbm, buf, sem, n_pages, rows):
    def wait(p, carry):
        _page_copy(cache_hbm, 0, buf, p, rows, sem).wait()
        return carry
    lax.fori_loop(0, n_pages, wait, 0)


def _compress_paged_kernel(pt_ref, kc_hbm, vc_hbm, pe_ref, wf_ref, ws_ref, w2_ref, ck_ref, cv_ref,
                           kbuf, vbuf, sems, *, n_pages):
    b = pl.program_id(0)
    rows = PAGE_SIZE // CMP_STRIDE
    _gather_pages(pt_ref, b, kc_hbm, kbuf, sems.at[0], n_pages, rows)
    _gather_pages(pt_ref, b, vc_hbm, vbuf, sems.at[1], n_pages, rows)
    for idx, (src_hbm, buf, dst) in enumerate(((kc_hbm, kbuf, ck_ref), (vc_hbm, vbuf, cv_ref))):
        _wait_pages(src_hbm, buf, sems.at[idx], n_pages, rows)
        dst[0] = _compress_units(buf[...], pe_ref[idx, 0:1, :], pe_ref[idx, 1:2, :],
                                 wf_ref[idx], ws_ref[idx], w2_ref[idx]).astype(BF16)


def _compress_paged(page_table, cache_k, cache_v, lw):
    batch, n_pages = page_table.shape
    n_phys = cache_k.shape[0]
    rows = PAGE_SIZE // CMP_STRIDE
    unit_w = CMP_STRIDE * KV_WIDTH
    n_units = n_pages * rows
    cache_k = cache_k.reshape(n_phys, rows, unit_w)
    cache_v = cache_v.reshape(n_phys, rows, unit_w)
    consts = [lw["cmp_pe"], lw["cmp_wf"], lw["cmp_ws"], lw["cmp_w2"]]
    const_spec = lambda a: pl.BlockSpec(a.shape, lambda b, pt: (0,) * a.ndim)
    any_spec = pl.BlockSpec(memory_space=pl.ANY)
    out_spec = pl.BlockSpec((1, n_units, KV_WIDTH), lambda b, pt: (b, 0, 0))
    grid_spec = pltpu.PrefetchScalarGridSpec(
        num_scalar_prefetch=1, grid=(batch,),
        in_specs=[any_spec, any_spec] + [const_spec(a) for a in consts],
        out_specs=[out_spec, out_spec],
        scratch_shapes=[pltpu.VMEM((n_units, unit_w), F32), pltpu.VMEM((n_units, unit_w), F32),
                        pltpu.SemaphoreType.DMA((2,))])
    return pl.pallas_call(
        functools.partial(_compress_paged_kernel, n_pages=n_pages),
        grid_spec=grid_spec,
        out_shape=[jax.ShapeDtypeStruct((batch, n_units, KV_WIDTH), BF16)] * 2,
        compiler_params=pltpu.CompilerParams(dimension_semantics=("arbitrary",), vmem_limit_bytes=VMEM_LIMIT_BYTES),
        name="compress_paged",
    )(page_table, cache_k, cache_v, *consts)


def _stack_heads(q):
    lane = _iota((q.shape[0], LANES), 1)
    lo = lane < HEAD_DIM
    c0, c1, c2 = (q[:, c * LANES:(c + 1) * LANES] for c in range(3))
    blocks = [jnp.where(lo, c0, 0.0), jnp.where(lo, pltpu.roll(c0, HEAD_DIM, 1), 0.0), jnp.where(lo, c1, 0.0),
              jnp.where(lo, 0.0, c1), jnp.where(lo, 0.0, pltpu.roll(c2, HEAD_DIM, 1)), jnp.where(lo, 0.0, c2)]
    return jnp.concatenate(blocks, axis=0)


def _unstack_heads(o, rows):
    lane = _iota((rows, LANES), 1)
    lo = lane < HEAD_DIM
    blk = [o[j * rows:(j + 1) * rows] for j in range(ATT_HEADS)]
    return jnp.concatenate([jnp.where(lo, blk[0], pltpu.roll(blk[1], HEAD_DIM, 1)),
                            jnp.where(lo, blk[2], blk[3]),
                            jnp.where(lo, pltpu.roll(blk[4], HEAD_DIM, 1), blk[5])], axis=1)


def _tile_heads(x, reps):
    return jnp.concatenate([x] * reps, axis=0)


def _masked_softmax(s, mask):
    s = jnp.where(mask, s, NEG)
    e = jnp.where(mask, jnp.exp(s - jnp.max(s, axis=1, keepdims=True)), 0.0)
    l = jnp.sum(e, axis=1, keepdims=True)
    return e * (1.0 / jnp.where(l > 0.0, l, 1.0))


def _block_importance(p_sum, m_imp):
    hi = p_sum.astype(BF16)
    lo = (p_sum - hi.astype(F32)).astype(BF16)
    return _dot(hi, m_imp) + _dot(lo, m_imp)


def _select_blocks(imp, tq):
    rows, width = imp.shape
    blk = _iota((rows, width), 1)
    cur = tq >> 6
    causal = blk * SEL_BLOCK <= tq
    forced = causal & ((blk == 0) | (blk == cur) | (blk == cur - 1))
    score = jnp.where(forced, FORCE, jnp.where(causal, imp, -FORCE))
    lane = blk.astype(F32)
    sel = jnp.zeros((rows, width), F32)
    for _ in range(N_SEL):
        mx = jnp.max(score, axis=1, keepdims=True)
        first = jnp.min(jnp.where(score == mx, lane, float(width)), axis=1, keepdims=True)
        pick = lane == first
        sel = jnp.where(pick & (mx > -FORCE / 2), 1.0, sel)
        score = jnp.where(pick, -jnp.inf, score)
    return sel


def _combine_gates(gates, o_c, o_s, o_w, rows):
    out = []
    for hh in range(ATT_HEADS):
        sl = slice(hh * rows, (hh + 1) * rows)
        g0 = GATE_LANE0 + 3 * hh
        out.append(gates[:, g0:g0 + 1] * o_c[sl] + gates[:, g0 + 1:g0 + 2] * o_s[sl] + gates[:, g0 + 2:g0 + 3] * o_w[sl])
    return jnp.concatenate(out, axis=0)


def _attn_prompt_kernel(qr_ref, qn_ref, small_ref, ck_ref, cv_ref, ks_ref, vs_ref, kw_ref, vw_ref,
                        onehot_ref, mimp_ref, out_ref, m_scr, l_scr, acc_scr, *, seq_len):
    R = Q_BLOCK
    i = pl.program_id(1)
    t0 = pl.multiple_of(i * R, R)
    scale = HEAD_DIM ** -0.5
    tq = t0 + _iota((R, 1), 0)
    tq_all = _tile_heads(tq, ATT_HEADS)
    q_rope = _stack_heads(qr_ref[...] * scale).astype(BF16)
    q_raw = _stack_heads(qn_ref[...] * scale).astype(BF16)

    n_units = ck_ref.shape[1]
    s_c = _dot_nt(q_raw, ck_ref[0])
    cmp_end = _iota((1, n_units), 1) * CMP_STRIDE + (CMP_LEN - 1)
    p_c = _masked_softmax(s_c, cmp_end <= tq_all)
    o_c = _dot(p_c.astype(BF16), cv_ref[0])

    bias = []
    for k in range(ATT_KV_HEADS):
        p_sum = sum(p_c[(k * GQA + g) * R:(k * GQA + g + 1) * R] for g in range(GQA))
        sel = _select_blocks(_block_importance(p_sum, mimp_ref[...]), tq)
        bias.append(_tile_heads(jnp.where(sel > 0.5, 0.0, NEG), GQA))
    q_aug = jnp.concatenate([q_rope, jnp.concatenate(bias, axis=0).astype(BF16)], axis=1)

    kd = ks_ref[pl.ds(t0, R), :]
    s_d = _dot_nt(q_rope, kd)
    causal_d = (t0 + _iota((1, R), 1)) <= tq_all
    s_d = jnp.where(causal_d, s_d, NEG)
    m0 = jnp.max(s_d, axis=1, keepdims=True)
    p_d = jnp.where(causal_d, jnp.exp(s_d - m0), 0.0)
    m_scr[...] = m0
    l_scr[...] = jnp.sum(p_d, axis=1, keepdims=True)
    acc_scr[...] = _dot(p_d.astype(BF16), vs_ref[pl.ds(t0, R), :])

    def tile_step(k0, width):
        k_aug = jnp.concatenate([ks_ref[pl.ds(k0, width), :], onehot_ref[pl.ds(k0, width), :]], axis=1)
        s = _dot_nt(q_aug, k_aug)
        m_old = m_scr[...]
        m_new = jnp.maximum(m_old, jnp.max(s, axis=1, keepdims=True))
        alpha = jnp.exp(m_old - m_new)
        p = jnp.exp(s - m_new)
        m_scr[...] = m_new
        l_scr[...] = alpha * l_scr[...] + jnp.sum(p, axis=1, keepdims=True)
        acc_scr[...] = alpha * acc_scr[...] + _dot(p.astype(BF16), vs_ref[pl.ds(k0, width), :])

    n_full = t0 // KEYS_PER_TILE

    def full_body(j, carry):
        tile_step(pl.multiple_of(j * KEYS_PER_TILE, KEYS_PER_TILE), KEYS_PER_TILE)
        return carry
    lax.fori_loop(0, n_full, full_body, 0)

    def rest_body(j, carry):
        tile_step(pl.multiple_of(n_full * KEYS_PER_TILE + j * R, R), R)
        return carry
    lax.fori_loop(0, (t0 - n_full * KEYS_PER_TILE) // R, rest_body, 0)
    o_s = acc_scr[...] * (1.0 / l_scr[...])

    span = WINDOW + R
    s0 = pl.multiple_of(jnp.maximum(t0 - WINDOW, 0), R)
    s_w = _dot_nt(q_rope, kw_ref[pl.ds(s0, span), :])
    spos = s0 + _iota((1, span), 1)
    p_w = _masked_softmax(s_w, (spos <= tq_all) & (spos > tq_all - WINDOW))
    o_w = _dot(p_w.astype(BF16), vw_ref[pl.ds(s0, span), :])

    out_ref[...] = _unstack_heads(_combine_gates(small_ref[...], o_c, o_s, o_w, R), R)


def _attn_prompt(qrope, qraw, small, ck, cv, kselb, vselb, kwinb, vwinb, *, batch, seq_len):
    assert seq_len % Q_BLOCK == 0 and seq_len >= WINDOW + Q_BLOCK and seq_len // SEL_BLOCK <= LANES
    nqb = seq_len // Q_BLOCK
    n_units = seq_len // CMP_STRIDE
    t = batch * seq_len
    pos = jnp.arange(seq_len, dtype=jnp.int32)
    onehot = (pos[:, None] // SEL_BLOCK == jnp.arange(LANES, dtype=jnp.int32)[None, :]).astype(BF16)
    m_imp = _importance_matrix(n_units, LANES)
    q_spec = lambda w: pl.BlockSpec((Q_BLOCK, w), lambda b, i: (b * nqb + i, 0))
    cmp_spec = pl.BlockSpec((1, n_units, KV_WIDTH), lambda b, i: (b, 0, 0))
    seq_spec = pl.BlockSpec((seq_len, KV_WIDTH), lambda b, i: (b, 0))
    const_spec = lambda a: pl.BlockSpec(a.shape, lambda b, i: (0, 0))
    rows = ATT_HEADS * Q_BLOCK
    return pl.pallas_call(
        functools.partial(_attn_prompt_kernel, seq_len=seq_len),
        grid=(batch, nqb),
        in_specs=[q_spec(ATT_WIDTH), q_spec(ATT_WIDTH), q_spec(LANES), cmp_spec, cmp_spec,
                  seq_spec, seq_spec, seq_spec, seq_spec, const_spec(onehot), const_spec(m_imp)],
        out_specs=q_spec(ATT_WIDTH),
        out_shape=jax.ShapeDtypeStruct((t, ATT_WIDTH), F32),
        scratch_shapes=[pltpu.VMEM((rows, 1), F32), pltpu.VMEM((rows, 1), F32), pltpu.VMEM((rows, LANES), F32)],
        compiler_params=pltpu.CompilerParams(dimension_semantics=("arbitrary", "arbitrary"),
                                             vmem_limit_bytes=VMEM_LIMIT_BYTES),
        name="attn_prompt",
    )(qrope, qraw, small, ck, cv, kselb, vselb, kwinb, vwinb, onehot, m_imp)


def _importance_matrix(n_units, width):
    n = jnp.arange(n_units, dtype=jnp.int32)[:, None]
    j = jnp.arange(width, dtype=jnp.int32)[None, :]
    upb = SEL_BLOCK // CMP_STRIDE
    m = (n // upb == j).astype(F32) + ((n + 1) // upb == j).astype(F32)
    return jnp.where(n < n_units - 1, m, 0.0).astype(BF16)


def _attn_sample_kernel(pt_ref, ks_hbm, vs_hbm, qr_ref, qn_ref, small_ref, ck_ref, cv_ref,
                        knew_ref, vnew_ref, kwc_ref, vwc_ref, kwnew_ref, vwnew_ref, onehot_ref, mimp_ref,
                        out_ref, kbuf, vbuf, sems, *, n_pages, past, win_buf, group_blocks):
    b = pl.program_id(0)
    R = qr_ref.shape[0]
    _gather_pages(pt_ref, b, ks_hbm, kbuf, sems.at[0], n_pages, PAGE_SIZE)
    _gather_pages(pt_ref, b, vs_hbm, vbuf, sems.at[1], n_pages, PAGE_SIZE)

    scale = HEAD_DIM ** -0.5
    tq = past + _iota((R, 1), 0)
    tq_all = _tile_heads(tq, ATT_HEADS)
    q_rope = _stack_heads(qr_ref[...] * scale).astype(BF16)
    q_raw = _stack_heads(qn_ref[...] * scale).astype(BF16)

    n_units = ck_ref.shape[1]
    s_c = _dot_nt(q_raw, ck_ref[0])
    cmp_end = _iota((1, n_units), 1) * CMP_STRIDE + (CMP_LEN - 1)
    p_c = _masked_softmax(s_c, cmp_end <= tq_all)
    o_c = _dot(p_c.astype(BF16), cv_ref[0])

    bias = []
    for k in range(ATT_KV_HEADS):
        p_sum = sum(p_c[(k * GQA + g) * R:(k * GQA + g + 1) * R] for g in range(GQA))
        sel = _select_blocks(_block_importance(p_sum, mimp_ref[...]), tq)
        bias.append(_tile_heads(jnp.where(sel > 0.5, 0.0, NEG), GQA))
    bias = jnp.concatenate(bias, axis=0).astype(BF16)

    k_off = past - win_buf
    s_wc = _dot_nt(q_rope, kwc_ref[0].astype(BF16))
    s_wn = _dot_nt(q_rope, kwnew_ref[...].astype(BF16))
    pos_c = k_off + _iota((1, win_buf), 1)
    pos_n = past + _iota((1, R), 1)
    mask_c = (pos_c <= tq_all) & (pos_c > tq_all - WINDOW) & (pos_c >= k_off)
    mask_n = (pos_n <= tq_all) & (pos_n > tq_all - WINDOW) & (pos_n >= k_off)
    s_wc = jnp.where(mask_c, s_wc, NEG)
    s_wn = jnp.where(mask_n, s_wn, NEG)
    m_w = jnp.maximum(jnp.max(s_wc, axis=1, keepdims=True), jnp.max(s_wn, axis=1, keepdims=True))
    e_wc = jnp.where(mask_c, jnp.exp(s_wc - m_w), 0.0)
    e_wn = jnp.where(mask_n, jnp.exp(s_wn - m_w), 0.0)
    l_w = jnp.sum(e_wc, axis=1, keepdims=True) + jnp.sum(e_wn, axis=1, keepdims=True)
    o_w = (_dot(e_wc.astype(BF16), vwc_ref[0].astype(BF16)) + _dot(e_wn.astype(BF16), vwnew_ref[...].astype(BF16)))
    o_w = o_w * (1.0 / jnp.where(l_w > 0.0, l_w, 1.0))

    s_n = _dot_nt(q_rope, knew_ref[...].astype(BF16))
    mask_sn = pos_n <= tq_all
    s_n = jnp.where(mask_sn, s_n, NEG)
    _wait_pages(ks_hbm, kbuf, sems.at[0], n_pages, PAGE_SIZE)
    group_keys = group_blocks * SEL_BLOCK
    n_groups = (n_pages * PAGE_SIZE) // group_keys
    s_past = []
    for gi in range(n_groups):
        q_aug = jnp.concatenate([q_rope, bias[:, gi * group_blocks:(gi + 1) * group_blocks]], axis=1)
        k_aug = jnp.concatenate([kbuf[gi * group_keys:(gi + 1) * group_keys, :].astype(BF16), onehot_ref[...]], axis=1)
        s_past.append(_dot_nt(q_aug, k_aug))
    m_s = jnp.max(s_n, axis=1, keepdims=True)
    for s in s_past:
        m_s = jnp.maximum(m_s, jnp.max(s, axis=1, keepdims=True))
    e_n = jnp.where(mask_sn, jnp.exp(s_n - m_s), 0.0)
    l_s = jnp.sum(e_n, axis=1, keepdims=True)
    o_s = _dot(e_n.astype(BF16), vnew_ref[...].astype(BF16))
    _wait_pages(vs_hbm, vbuf, sems.at[1], n_pages, PAGE_SIZE)
    for gi, s in enumerate(s_past):
        e = jnp.exp(s - m_s)
        l_s = l_s + jnp.sum(e, axis=1, keepdims=True)
        o_s = o_s + _dot(e.astype(BF16), vbuf[gi * group_keys:(gi + 1) * group_keys, :].astype(BF16))
    o_s = o_s * (1.0 / l_s)

    out_ref[...] = _unstack_heads(_combine_gates(small_ref[...], o_c, o_s, o_w, R), R)


def _attn_sample(page_table, cache_ks, cache_vs, cache_kw, cache_vw, qrope, qraw, small, ck, cv,
                 knew, vnew, kwnew, vwnew, *, seq_len):
    batch, n_pages = page_table.shape
    past = n_pages * PAGE_SIZE
    win_buf = cache_kw.shape[1]
    assert seq_len == SUBLANES and (past + seq_len) // CMP_STRIDE == past // CMP_STRIDE and past >= WINDOW
    n_phys = cache_ks.shape[0]
    n_units = past // CMP_STRIDE
    nb_past = past // SEL_BLOCK
    group_blocks = min(LANES, nb_past)
    assert nb_past % group_blocks == 0
    n_blk_lanes = -(-(nb_past + 1) // LANES) * LANES
    cache_ks = cache_ks.reshape(n_phys, PAGE_SIZE, KV_WIDTH)
    cache_vs = cache_vs.reshape(n_phys, PAGE_SIZE, KV_WIDTH)
    cache_kw = cache_kw.reshape(batch, win_buf, KV_WIDTH)
    cache_vw = cache_vw.reshape(batch, win_buf, KV_WIDTH)
    key = jnp.arange(group_blocks * SEL_BLOCK, dtype=jnp.int32)
    onehot = (key[:, None] // SEL_BLOCK == jnp.arange(group_blocks, dtype=jnp.int32)[None, :]).astype(BF16)
    m_imp = _importance_matrix(n_units, n_blk_lanes)
    any_spec = pl.BlockSpec(memory_space=pl.ANY)
    row_spec = lambda w: pl.BlockSpec((seq_len, w), lambda b, pt: (b, 0))
    batch_spec = lambda a: pl.BlockSpec((1,) + a.shape[1:], lambda b, pt: (b, 0, 0))
    const_spec = lambda a: pl.BlockSpec(a.shape, lambda b, pt: (0, 0))
    grid_spec = pltpu.PrefetchScalarGridSpec(
        num_scalar_prefetch=1, grid=(batch,),
        in_specs=[any_spec, any_spec, row_spec(ATT_WIDTH), row_spec(ATT_WIDTH), row_spec(LANES),
                  batch_spec(ck), batch_spec(cv), row_spec(KV_WIDTH), row_spec(KV_WIDTH),
                  batch_spec(cache_kw), batch_spec(cache_vw), row_spec(KV_WIDTH), row_spec(KV_WIDTH),
                  const_spec(onehot), const_spec(m_imp)],
        out_specs=row_spec(ATT_WIDTH),
        scratch_shapes=[pltpu.VMEM((past, KV_WIDTH), F32), pltpu.VMEM((past, KV_WIDTH), F32),
                        pltpu.SemaphoreType.DMA((2,))])
    return pl.pallas_call(
        functools.partial(_attn_sample_kernel, n_pages=n_pages, past=past, win_buf=win_buf, group_blocks=group_blocks),
        grid_spec=grid_spec,
        out_shape=jax.ShapeDtypeStruct((batch * seq_len, ATT_WIDTH), F32),
        compiler_params=pltpu.CompilerParams(dimension_semantics=("arbitrary",), vmem_limit_bytes=VMEM_LIMIT_BYTES),
        name="attn_sample",
    )(page_table, cache_ks, cache_vs, qrope, qraw, small, ck, cv, knew, vnew, cache_kw, cache_vw, kwnew, vwnew,
      onehot, m_imp)


def _ffn_kernel(h_ref, yssd_ref, ysc_ref, yatt_ref, wo_ref, n2_ref, w1_ref, w2_ref, fn_ref, out_ref,
                xn_scr, hs_scr, acc_scr, *, n_ff, final):
    j = pl.program_id(1)

    @pl.when(j == 0)
    def _():
        mix = (_dot(yssd_ref[...].astype(BF16), wo_ref[0:SSD_WIDTH, :])
               + _dot(ysc_ref[...].astype(BF16), wo_ref[SSD_WIDTH:SSD_WIDTH + SC_WIDTH, :])
               + _dot(yatt_ref[...].astype(BF16), wo_ref[SSD_WIDTH + SC_WIDTH:D_MODEL, :]))
        hn = h_ref[...] + mix
        hs_scr[...] = hn
        xn_scr[...] = _rms(hn, n2_ref[...]).astype(BF16)
        acc_scr[...] = jnp.zeros_like(acc_scr)

    hid = jnp.square(jnp.maximum(_dot(xn_scr[...], w1_ref[...]), 0.0))
    acc_scr[...] += _dot(hid.astype(BF16), w2_ref[...])

    @pl.when(j == n_ff - 1)
    def _():
        o = hs_scr[...] + acc_scr[...]
        out_ref[...] = _rms(o, fn_ref[...]) if final else o


def _ffn(h, yssd, ysc, yatt, lw, final_norm_w, *, final):
    t = h.shape[0]
    tm = min(512, t)
    tf = 512
    assert t % tm == 0 and D_FF % tf == 0
    n_ff = D_FF // tf
    row_spec = lambda w: pl.BlockSpec((tm, w), lambda i, j: (i, 0))
    const_spec = lambda a: pl.BlockSpec(a.shape, lambda i, j: (0, 0))
    return pl.pallas_call(
        functools.partial(_ffn_kernel, n_ff=n_ff, final=final),
        grid=(t // tm, n_ff),
        in_specs=[row_spec(D_MODEL), row_spec(SSD_WIDTH), row_spec(SC_WIDTH), row_spec(ATT_WIDTH),
                  const_spec(lw["w_out"]), const_spec(lw["norm2_w"]),
                  pl.BlockSpec((D_MODEL, tf), lambda i, j: (0, j)), pl.BlockSpec((tf, D_MODEL), lambda i, j: (j, 0)),
                  const_spec(final_norm_w)],
        out_specs=row_spec(D_MODEL),
        out_shape=jax.ShapeDtypeStruct((t, D_MODEL), F32),
        scratch_shapes=[pltpu.VMEM((tm, D_MODEL), BF16), pltpu.VMEM((tm, D_MODEL), F32), pltpu.VMEM((tm, D_MODEL), F32)],
        compiler_params=pltpu.CompilerParams(dimension_semantics=("arbitrary", "arbitrary"),
                                             vmem_limit_bytes=VMEM_LIMIT_BYTES),
        name="ffn",
    )(h, yssd, ysc, yatt, lw["w_out"], lw["norm2_w"], lw["w_ff1"], lw["w_ff2"], final_norm_w)


def _lane_row(v, width):
    v = v.reshape(1, -1).astype(F32)
    return jnp.pad(v, ((0, 0), (0, width - v.shape[1])))


def _prep_layer(l, norm1_w, w_in, ssd_conv_w, ssd_conv_b, ssd_dt_bias, ssd_a_log, ssd_d, ssd_norm_w, sc_conv_w,
                cmp_pe, cmp_w1, cmp_w2, w_out, norm2_w, w_ff1, w_ff2):
    offs = [0]
    for s in IN_SIZES:
        offs.append(offs[-1] + s)
    seg = lambda a: jnp.arange(offs[a], offs[a + 1])
    perm = jnp.concatenate([seg(0), seg(1), seg(3), seg(4), seg(5), seg(6)] + [seg(a) for a in range(7, 13)]
                           + [seg(2), seg(13)])
    w_perm = jnp.pad(w_in[l][:, perm], ((0, 0), (0, IN_PAD - offs[-1]))).astype(BF16)

    eye = jnp.eye(ATT_KV_HEADS, dtype=F32)
    w1 = cmp_w1[l].reshape(2, 2, CMP_STRIDE, HEAD_DIM, HEAD_DIM)
    wfs = jnp.einsum('vhrde,kK->vhrkdKe', w1, eye).reshape(2, 2, CMP_STRIDE * KV_WIDTH, KV_WIDTH).astype(BF16)
    pe = cmp_pe[l].reshape(2, 2, CMP_STRIDE, 1, HEAD_DIM)
    pe = jnp.broadcast_to(pe, (2, 2, CMP_STRIDE, ATT_KV_HEADS, HEAD_DIM)).reshape(2, 2, CMP_STRIDE * KV_WIDTH)
    w2 = jnp.einsum('vde,kK->vkdKe', cmp_w2[l], eye).reshape(2, KV_WIDTH, KV_WIDTH).astype(BF16)

    head = jnp.arange(LANES, dtype=jnp.int32)[:, None]
    col = jnp.arange(SSD_WIDTH, dtype=jnp.int32)[None, :]
    idx = jnp.arange(SSD_CHUNK, dtype=jnp.int32)
    return {
        "norm1_w": norm1_w[l].reshape(1, D_MODEL), "w_in": w_perm,
        "ssd_conv_w": ssd_conv_w[l], "ssd_conv_b": ssd_conv_b[l].reshape(1, SSD_XBC),
        "dt_bias": _lane_row(ssd_dt_bias[l], LANES), "a_log": _lane_row(ssd_a_log[l], LANES),
        "d_skip": jnp.repeat(ssd_d[l], HEAD_DIM).reshape(1, SSD_WIDTH),
        "ssd_norm_w": ssd_norm_w[l].reshape(1, SSD_WIDTH), "sc_conv_w": sc_conv_w[l],
        "tril": (idx[:, None] >= idx[None, :]).astype(F32),
        "expand": (col // HEAD_DIM == head).astype(F32),
        "cmp_pe": pe, "cmp_wf": wfs[:, 0], "cmp_ws": wfs[:, 1], "cmp_w2": w2,
        "w_out": w_out[l].astype(BF16), "norm2_w": norm2_w[l].reshape(1, D_MODEL),
        "w_ff1": w_ff1[l].astype(BF16), "w_ff2": w_ff2[l].astype(BF16),
    }


def _front_pad_rows(a, rows):
    return jnp.pad(a, ((0, 0), (rows - a.shape[1], 0), (0, 0)))


def _run_layer(h, lw, invf, ssm0, cssd0, csc0, attn_fn, final_norm_w, *, batch, seq_len, pos0, final):
    (z, xbc, sc, qraw, qrope, kcmp, vcmp, ksel, vsel, kwin, vwin, small,
     kselb, vselb, kwinb, vwinb) = _inproj(h, lw["norm1_w"], lw["w_in"], invf, seq_len=seq_len, pos0=pos0)
    yssd, ysc, ssm_new, cssd_new, csc_new = _mixer(
        z, xbc, sc, small, ssm0, _front_pad_rows(cssd0, SUBLANES), _front_pad_rows(csc0, SUBLANES), lw,
        batch=batch, seq_len=seq_len)
    rows = dict(kcmp=kcmp, vcmp=vcmp, ksel=ksel, vsel=vsel, kwin=kwin, vwin=vwin,
                kselb=kselb, vselb=vselb, kwinb=kwinb, vwinb=vwinb)
    yatt = attn_fn(qrope, qraw, small, rows)
    h = _ffn(h, yssd, ysc, yatt, lw, final_norm_w, final=final)
    states = (ssm_new, cssd_new[:, SUBLANES - (SSD_CONV - 1):], csc_new[:, SUBLANES - (SC_CONV - 1):])
    return h, states, rows


def kernel(x_prompt, x_sample, cache_k_cmp, cache_v_cmp, cache_k_sel, cache_v_sel, cache_k_win, cache_v_win, state_ssm, state_ssd_conv, state_sc_conv, page_table, norm1_w, w_in, ssd_conv_w, ssd_conv_b, ssd_dt_bias, ssd_a_log, ssd_d, ssd_norm_w, sc_conv_w, cmp_pe, cmp_w1, cmp_w2, w_out, norm2_w, w_ff1, w_ff2, final_norm_w):
    bp, lp, _ = x_prompt.shape
    db, ls, _ = x_sample.shape
    depth = w_in.shape[0]
    n_pages = page_table.shape[1]
    past = n_pages * PAGE_SIZE
    win_buf = cache_k_win.shape[2]
    fnw = final_norm_w.reshape(1, D_MODEL)

    half = ROT_DIM // 2
    inv_freq = ROPE_THETA ** (-jnp.arange(half, dtype=F32) * 2.0 / ROT_DIM)
    invf = jnp.tile(inv_freq, LANES // half).reshape(1, LANES)

    hp = x_prompt.reshape(bp * lp, D_MODEL)
    hs = x_sample.reshape(db * ls, D_MODEL)
    ssm0_p = jnp.zeros((bp, SSD_HEADS, HEAD_DIM, SSD_STATE), F32)
    cssd0_p = jnp.zeros((bp, SSD_CONV - 1, SSD_XBC), F32)
    csc0_p = jnp.zeros((bp, SC_CONV - 1, SC_WIDTH), F32)
    p_lists = [[] for _ in range(9)]
    s_lists = [[] for _ in range(9)]
    keep_p = min(WINDOW, lp)
    keep_s = min(WINDOW, win_buf + ls)
    kv4 = lambda a, b, n: a.reshape(b, n, ATT_KV_HEADS, HEAD_DIM)

    for l in range(depth):
        lw = _prep_layer(l, norm1_w, w_in, ssd_conv_w, ssd_conv_b, ssd_dt_bias, ssd_a_log, ssd_d, ssd_norm_w,
                         sc_conv_w, cmp_pe, cmp_w1, cmp_w2, w_out, norm2_w, w_ff1, w_ff2)
        final = l == depth - 1

        def prompt_attn(qrope, qraw, small, rows, lw=lw):
            ck, cv = _compress_prompt(rows["kcmp"], rows["vcmp"], lw, batch=bp, seq_len=lp)
            return _attn_prompt(qrope, qraw, small, ck, cv, rows["kselb"], rows["vselb"], rows["kwinb"], rows["vwinb"],
                                batch=bp, seq_len=lp)

        def sample_attn(qrope, qraw, small, rows, lw=lw, l=l):
            ck, cv = _compress_paged(page_table, cache_k_cmp[l], cache_v_cmp[l], lw)
            return _attn_sample(page_table, cache_k_sel[l], cache_v_sel[l], cache_k_win[l], cache_v_win[l],
                                qrope, qraw, small, ck, cv, rows["ksel"], rows["vsel"], rows["kwin"], rows["vwin"],
                                seq_len=ls)

        hp, st_p, rows_p = _run_layer(hp, lw, invf, ssm0_p, cssd0_p, csc0_p, prompt_attn, fnw,
                                      batch=bp, seq_len=lp, pos0=0, final=final)
        hs, st_s, rows_s = _run_layer(hs, lw, invf, state_ssm[l], state_ssd_conv[l], state_sc_conv[l], sample_attn, fnw,
                                      batch=db, seq_len=ls, pos0=past, final=final)
        p_new = [kv4(rows_p[n], bp, lp) for n in ("kcmp", "vcmp", "ksel", "vsel")]
        p_new += [kv4(rows_p[n], bp, lp)[:, -keep_p:] for n in ("kwin", "vwin")]
        p_new += list(st_p)
        s_new = [kv4(rows_s[n], db, ls) for n in ("kcmp", "vcmp", "ksel", "vsel")]
        s_new += [jnp.concatenate([cache_k_win[l], kv4(rows_s["kwin"], db, ls)], axis=1)[:, -keep_s:],
                  jnp.concatenate([cache_v_win[l], kv4(rows_s["vwin"], db, ls)], axis=1)[:, -keep_s:]]
        s_new += list(st_s)
        for lst, arr in zip(p_lists, p_new):
            lst.append(arr)
        for lst, arr in zip(s_lists, s_new):
            lst.append(arr)

    p_out = [jnp.stack(a) for a in p_lists]
    s_out = [jnp.stack(a) for a in s_lists]
    y_prompt = hp.reshape(bp, lp, D_MODEL)
    y_sample = hs.reshape(db, ls, D_MODEL)
    return (y_prompt, y_sample, *p_out, *s_out)
```

```python
import functools
import math

import jax
import jax.numpy as jnp
from jax import lax
from jax.experimental import pallas as pl
from jax.experimental.pallas import tpu as pltpu

F32 = jnp.float32
BF16 = jnp.bfloat16
HIGHEST = lax.Precision.HIGHEST

D_MODEL = 1024
HEAD_DIM = 64
SSD_WIDTH = 384
SSD_HEADS = 6
SSD_GROUPS = 2
SSD_STATE = 128
SSD_CONV = 4
SSD_CHUNK = 128
SSD_XBC = SSD_WIDTH + 2 * SSD_GROUPS * SSD_STATE
SC_WIDTH = 256
SC_CONV = 3
ATT_WIDTH = 384
ATT_HEADS = 6
ATT_KV_HEADS = 2
GQA = ATT_HEADS // ATT_KV_HEADS
KV_WIDTH = ATT_KV_HEADS * HEAD_DIM
CMP_LEN = 32
CMP_STRIDE = 16
SEL_BLOCK = 64
N_SEL = 16
WINDOW = 512
Q_BLOCK = 128
PAGE_SIZE = 128
ROPE_THETA = 500000.0
ROT_DIM = HEAD_DIM // 4
D_FF = 4 * D_MODEL
EPS = 1e-6
NEG = -1e30
FORCE = 1e4
IN_SIZES = (SSD_WIDTH, SSD_XBC, SSD_HEADS, SC_WIDTH, SC_WIDTH, SC_WIDTH, ATT_WIDTH,
            KV_WIDTH, KV_WIDTH, KV_WIDTH, KV_WIDTH, KV_WIDTH, KV_WIDTH, ATT_HEADS * 3)

LANES = 128
SUBLANES = 8
VMEM_LIMIT_BYTES = 56 * 1024 * 1024

COL_Z = 0
COL_XBC = COL_Z + SSD_WIDTH
COL_SC = COL_XBC + SSD_XBC
COL_Q = COL_SC + 3 * SC_WIDTH
COL_KV = COL_Q + ATT_WIDTH
COL_SMALL = COL_KV + 6 * KV_WIDTH
IN_PAD = COL_SMALL + LANES
GATE_LANE0 = SSD_HEADS
KEYS_PER_TILE = 512
INPROJ_ROWS = 512
FFN_ROWS = 1024
FFN_HIDDEN_TILE = 512


def _dot(a, b):
    return jnp.dot(a, b, preferred_element_type=F32)


def _dot_nt(a, b):
    return lax.dot_general(a, b, (((1,), (1,)), ((), ())), preferred_element_type=F32)


def _dot_exact(a, b):
    return jnp.dot(a, b, preferred_element_type=F32, precision=HIGHEST)


def _silu(x):
    return x * (1.0 / (1.0 + jnp.exp(-x)))


def _sigmoid(x):
    return 1.0 / (1.0 + jnp.exp(-x))


def _rms(x, w):
    return x * lax.rsqrt(jnp.mean(x * x, axis=-1, keepdims=True) + EPS) * w


def _iota(shape, dim):
    return lax.broadcasted_iota(jnp.int32, shape, dim)


def _rope_chunk(x, cos_t, sin_t, ll):
    fwd = pltpu.roll(x, LANES - ROT_DIM // 2, 1)
    bwd = pltpu.roll(x, ROT_DIM // 2, 1)
    partner = jnp.where(ll < ROT_DIM // 2, fwd, bwd)
    return x * cos_t + partner * sin_t


def _inproj_kernel(x_ref, nw_ref, w_ref, invf_ref,
                   z_ref, xbc_ref, sc_ref, qraw_ref, qrope_ref,
                   kcmp_ref, vcmp_ref, ksel_ref, vsel_ref, kwin_ref, vwin_ref, small_ref,
                   kcmp_rows_ref, vcmp_rows_ref, kselb_ref, vselb_ref, kwinb_ref, vwinb_ref,
                   *, tm, seq_len, pos0, dims_major):
    i = pl.program_id(0)
    xn = _rms(x_ref[...], nw_ref[...]).astype(BF16)

    def proj(c0, width):
        return _dot(xn, w_ref[:, c0:c0 + width])

    z_ref[...] = proj(COL_Z, SSD_WIDTH)
    xbc_ref[...] = proj(COL_XBC, SSD_XBC)
    sc_ref[...] = proj(COL_SC, 3 * SC_WIDTH)

    row = i * tm + _iota((tm, LANES), 0)
    pos = (pos0 + (row & (seq_len - 1))).astype(F32)
    lane = _iota((tm, LANES), 1)
    ll = lane & (HEAD_DIM - 1)
    ang = pos * invf_ref[...]
    cos_a = jnp.cos(ang)
    sin_a = jnp.sin(ang)
    half = ROT_DIM // 2
    cos_t = jnp.where(ll < ROT_DIM, cos_a, 1.0)
    sin_t = jnp.where(ll < half, -sin_a, jnp.where(ll < ROT_DIM, sin_a, 0.0))

    q = proj(COL_Q, ATT_WIDTH)
    qraw_ref[...] = q
    qrope_ref[...] = jnp.concatenate(
        [_rope_chunk(q[:, c * LANES:(c + 1) * LANES], cos_t, sin_t, ll) for c in range(ATT_WIDTH // LANES)], axis=1)

    kv = proj(COL_KV, 6 * KV_WIDTH)
    kcmp = kv[:, 0:LANES]
    vcmp = kv[:, LANES:2 * LANES]
    ksel = _rope_chunk(kv[:, 2 * LANES:3 * LANES], cos_t, sin_t, ll)
    vsel = kv[:, 3 * LANES:4 * LANES]
    kwin = _rope_chunk(kv[:, 4 * LANES:5 * LANES], cos_t, sin_t, ll)
    vwin = kv[:, 5 * LANES:6 * LANES]
    for ref, val in ((kcmp_ref, kcmp), (vcmp_ref, vcmp), (ksel_ref, ksel), (vsel_ref, vsel),
                     (kwin_ref, kwin), (vwin_ref, vwin)):
        if dims_major:
            ref[0] = val.T
        else:
            ref[...] = val
    kcmp_rows_ref[...] = kcmp
    vcmp_rows_ref[...] = vcmp
    kselb_ref[...] = ksel.astype(BF16)
    vselb_ref[...] = vsel.astype(BF16)
    kwinb_ref[...] = kwin.astype(BF16)
    vwinb_ref[...] = vwin.astype(BF16)

    sm = proj(COL_SMALL, LANES)
    is_gate = (lane >= GATE_LANE0) & (lane < GATE_LANE0 + 3 * ATT_HEADS)
    small_ref[...] = jnp.where(is_gate, _sigmoid(sm), sm)


def _inproj(h, norm_w, w_perm, invf, *, batch, seq_len, pos0):
    t = h.shape[0]
    tm = min(INPROJ_ROWS, t)
    assert t % tm == 0 and seq_len & (seq_len - 1) == 0
    dims_major = seq_len % tm == 0
    tiles_per_row = max(seq_len // tm, 1)
    row_spec = lambda w: pl.BlockSpec((tm, w), lambda i: (i, 0))
    const_spec = lambda a: pl.BlockSpec(a.shape, lambda i: (0, 0))
    row_out = lambda w, dt: (row_spec(w), jax.ShapeDtypeStruct((t, w), dt))
    if dims_major:
        kv_out = (pl.BlockSpec((1, KV_WIDTH, tm), lambda i: (i // tiles_per_row, 0, i % tiles_per_row)),
                  jax.ShapeDtypeStruct((batch, KV_WIDTH, seq_len), F32))
    else:
        kv_out = row_out(KV_WIDTH, F32)
    outs = ([row_out(w, F32) for w in (SSD_WIDTH, SSD_XBC, 3 * SC_WIDTH, ATT_WIDTH, ATT_WIDTH)] + [kv_out] * 6
            + [row_out(LANES, F32)] + [row_out(KV_WIDTH, F32)] * 2 + [row_out(KV_WIDTH, BF16)] * 4)
    return pl.pallas_call(
        functools.partial(_inproj_kernel, tm=tm, seq_len=seq_len, pos0=pos0, dims_major=dims_major),
        grid=(t // tm,),
        in_specs=[row_spec(D_MODEL), const_spec(norm_w), const_spec(w_perm), const_spec(invf)],
        out_specs=[o[0] for o in outs],
        out_shape=[o[1] for o in outs],
        compiler_params=pltpu.CompilerParams(dimension_semantics=("arbitrary",), vmem_limit_bytes=VMEM_LIMIT_BYTES),
        name="inproj",
    )(h, norm_w, w_perm, invf)


def _pad_rows(x, rows):
    if x.shape[0] == rows:
        return x
    return jnp.concatenate([x, jnp.zeros((rows - x.shape[0], x.shape[1]), x.dtype)], axis=0)


def _mixer_kernel(z_ref, xbc_ref, sc_ref, small_ref, ssm0_ref, cssd0_ref, csc0_ref,
                  cw_ref, cb_ref, dtb_ref, alog_ref, dskip_ref, nw_ref, scw_ref, tril_ref, expand_ref,
                  yssd_ref, ysc_ref, ssm_out_ref, cssd_out_ref, csc_out_ref,
                  s_scr, xp_scr, scp_scr, *, cs, n_chunks):
    C = SSD_CHUNK
    c = pl.program_id(1)

    @pl.when(c == 0)
    def _():
        s_scr[...] = ssm0_ref[0]
        xp_scr[0:SUBLANES, :] = cssd0_ref[0]
        scp_scr[0:SUBLANES, :] = csc0_ref[0]

    xp_scr[SUBLANES:SUBLANES + C, :] = _pad_rows(xbc_ref[...], C)
    cw = cw_ref[...]
    conv = cb_ref[...]
    for k in range(SSD_CONV):
        off = SUBLANES - (SSD_CONV - 1) + k
        conv = conv + cw[k:k + 1, :] * xp_scr[off:off + C, :]
    xbc_c = _silu(conv)
    tail_ssd = xp_scr[cs:cs + SUBLANES, :]
    xp_scr[0:SUBLANES, :] = tail_ssd

    xs = xbc_c[:, 0:SSD_WIDTH]
    bm = xbc_c[:, SSD_WIDTH:SSD_WIDTH + SSD_GROUPS * SSD_STATE]
    cm = xbc_c[:, SSD_WIDTH + SSD_GROUPS * SSD_STATE:SSD_XBC]

    lane = _iota((C, LANES), 1)
    rowi = _iota((C, LANES), 0)
    small = _pad_rows(small_ref[...], C)
    dt_raw = small + dtb_ref[...]
    dt = jnp.maximum(dt_raw, 0.0) + jnp.log1p(jnp.exp(-jnp.abs(dt_raw)))
    dt = jnp.where((lane < SSD_HEADS) & (rowi < cs), dt, 0.0)
    a_row = jnp.where(lane[0:1, :] < SSD_HEADS, -jnp.exp(alog_ref[...]), 0.0)
    acum = _dot_exact(tril_ref[...], dt * a_row)
    acum_t = acum.T
    a_last = acum[C - 1:C, :]
    expand = expand_ref[...]
    dt_w = _dot_exact(dt, expand)
    eacc_w = _dot_exact(jnp.exp(acum), expand)
    dend_w = _dot_exact(jnp.exp(a_last - acum), expand)
    cdec = jnp.exp(a_last)
    xdt = xs * dt_w
    xdtw_t = (xdt * dend_w).T

    causal = _iota((C, C), 0) >= _iota((C, C), 1)
    head_lo = lane < HEAD_DIM
    zeros_half = jnp.zeros((HEAD_DIM, SSD_STATE), BF16)
    y_chunks = [None] * (SSD_WIDTH // LANES)
    for g in range(SSD_GROUPS):
        bm_g = bm[:, g * SSD_STATE:(g + 1) * SSD_STATE].astype(BF16)
        cm_g = cm[:, g * SSD_STATE:(g + 1) * SSD_STATE].astype(BF16)
        cb = _dot_nt(cm_g, bm_g)
        for r in range(SSD_HEADS // SSD_GROUPS):
            h = g * (SSD_HEADS // SSD_GROUPS) + r
            ch, lo = h // 2, h % 2 == 0
            keep = head_lo if lo else jnp.logical_not(head_lo)
            seg = acum[:, h:h + 1] - acum_t[h:h + 1, :]
            decay = jnp.where(causal, jnp.exp(jnp.where(causal, seg, 0.0)), 0.0)
            xdt_h = jnp.where(keep, xdt[:, ch * LANES:(ch + 1) * LANES], 0.0).astype(BF16)
            y_d = _dot((cb * decay).astype(BF16), xdt_h)
            s_h = s_scr[h]
            s_b = s_h.astype(BF16)
            s_pl = jnp.concatenate([s_b, zeros_half] if lo else [zeros_half, s_b], axis=0)
            y_o = _dot_nt(cm_g, s_pl)
            y_h = y_d + eacc_w[:, ch * LANES:(ch + 1) * LANES] * y_o
            y_chunks[ch] = y_h if y_chunks[ch] is None else y_chunks[ch] + y_h
            s_scr[h] = s_h * cdec[:, h:h + 1] + _dot(xdtw_t[h * HEAD_DIM:(h + 1) * HEAD_DIM, :].astype(BF16), bm_g)
    y = jnp.concatenate(y_chunks, axis=1) + dskip_ref[...] * xs
    gated = y * _silu(_pad_rows(z_ref[...], C))
    yssd_ref[...] = _rms(gated, nw_ref[...])[0:cs]

    sc = _pad_rows(sc_ref[...], C)
    scp_scr[SUBLANES:SUBLANES + C, :] = sc[:, SC_WIDTH:2 * SC_WIDTH] * sc[:, 2 * SC_WIDTH:3 * SC_WIDTH]
    scw = scw_ref[...]
    conv3 = jnp.zeros((C, SC_WIDTH), F32)
    for k in range(SC_CONV):
        off = SUBLANES - (SC_CONV - 1) + k
        conv3 = conv3 + scw[k:k + 1, :] * scp_scr[off:off + C, :]
    ysc_ref[...] = (sc[:, 0:SC_WIDTH] * conv3)[0:cs]
    tail_sc = scp_scr[cs:cs + SUBLANES, :]
    scp_scr[0:SUBLANES, :] = tail_sc

    @pl.when(c == n_chunks - 1)
    def _():
        ssm_out_ref[0] = s_scr[...]
        cssd_out_ref[0] = tail_ssd
        csc_out_ref[0] = tail_sc


def _mixer(z, xbc, sc, small, ssm0, cssd0, csc0, lw, *, batch, seq_len):
    cs = min(SSD_CHUNK, seq_len)
    assert seq_len % cs == 0 and cs % SUBLANES == 0
    nc = seq_len // cs
    t = batch * seq_len
    row_spec = lambda w: pl.BlockSpec((cs, w), lambda b, c: (b * nc + c, 0))
    const_spec = lambda a: pl.BlockSpec(a.shape, lambda b, c: (0,) * a.ndim)
    batch_spec = lambda a: pl.BlockSpec((1,) + a.shape[1:], lambda b, c: (b,) + (0,) * (a.ndim - 1))
    consts = [lw["ssd_conv_w"], lw["ssd_conv_b"], lw["dt_bias"], lw["a_log"], lw["d_skip"], lw["ssd_norm_w"],
              lw["sc_conv_w"], lw["tril"], lw["expand"]]
    out_shape = [jax.ShapeDtypeStruct((t, SSD_WIDTH), F32), jax.ShapeDtypeStruct((t, SC_WIDTH), F32),
                 jax.ShapeDtypeStruct(ssm0.shape, F32), jax.ShapeDtypeStruct(cssd0.shape, F32),
                 jax.ShapeDtypeStruct(csc0.shape, F32)]
    return pl.pallas_call(
        functools.partial(_mixer_kernel, cs=cs, n_chunks=nc),
        grid=(batch, nc),
        in_specs=[row_spec(SSD_WIDTH), row_spec(SSD_XBC), row_spec(3 * SC_WIDTH), row_spec(LANES),
                  batch_spec(ssm0), batch_spec(cssd0), batch_spec(csc0)] + [const_spec(a) for a in consts],
        out_specs=[row_spec(SSD_WIDTH), row_spec(SC_WIDTH), batch_spec(ssm0), batch_spec(cssd0), batch_spec(csc0)],
        out_shape=out_shape,
        scratch_shapes=[pltpu.VMEM((SSD_HEADS, HEAD_DIM, SSD_STATE), F32),
                        pltpu.VMEM((SUBLANES + SSD_CHUNK, SSD_XBC), F32),
                        pltpu.VMEM((SUBLANES + SSD_CHUNK, SC_WIDTH), F32)],
        compiler_params=pltpu.CompilerParams(dimension_semantics=("arbitrary", "arbitrary"),
                                             vmem_limit_bytes=VMEM_LIMIT_BYTES),
        name="mixer",
    )(z, xbc, sc, small, ssm0, cssd0, csc0, *consts)


def _compress_units(units, pe0, pe1, w_first, w_second):
    n_units = units.shape[0]
    first = _dot((units + pe0).astype(BF16), w_first)
    second = _dot((units + pe1).astype(BF16), w_second)
    return _silu(first + pltpu.roll(second, n_units - 1, 0))


def _compress_prompt_kernel(kc_ref, vc_ref, pe_ref, wf_ref, ws_ref, w2_ref, w2t_ref, ck_ref, cvt_ref):
    hid_k = _compress_units(kc_ref[0], pe_ref[0, 0:1, :], pe_ref[0, 1:2, :], wf_ref[0], ws_ref[0])
    ck_ref[0] = _dot(hid_k.astype(BF16), w2_ref[0]).astype(BF16)
    hid_v = _compress_units(vc_ref[0], pe_ref[1, 0:1, :], pe_ref[1, 1:2, :], wf_ref[1], ws_ref[1])
    cvt_ref[0] = _dot(w2t_ref[1], hid_v.T.astype(BF16)).astype(BF16)


def _compress_prompt(kc, vc, lw, *, batch, seq_len):
    n_units = seq_len // CMP_STRIDE
    unit_w = CMP_STRIDE * KV_WIDTH
    kc = kc.reshape(batch, n_units, unit_w)
    vc = vc.reshape(batch, n_units, unit_w)
    consts = [lw["cmp_pe"], lw["cmp_wf"], lw["cmp_ws"], lw["cmp_w2"], lw["cmp_w2t"]]
    const_spec = lambda a: pl.BlockSpec(a.shape, lambda b: (0,) * a.ndim)
    in_spec = pl.BlockSpec((1, n_units, unit_w), lambda b: (b, 0, 0))
    return pl.pallas_call(
        _compress_prompt_kernel,
        grid=(batch,),
        in_specs=[in_spec, in_spec] + [const_spec(a) for a in consts],
        out_specs=[pl.BlockSpec((1, n_units, KV_WIDTH), lambda b: (b, 0, 0)),
                   pl.BlockSpec((1, KV_WIDTH, n_units), lambda b: (b, 0, 0))],
        out_shape=[jax.ShapeDtypeStruct((batch, n_units, KV_WIDTH), BF16),
                   jax.ShapeDtypeStruct((batch, KV_WIDTH, n_units), BF16)],
        compiler_params=pltpu.CompilerParams(dimension_semantics=("arbitrary",), vmem_limit_bytes=VMEM_LIMIT_BYTES),
        name="compress_prompt",
    )(kc, vc, *consts)


def _page_copy(cache_hbm, layer, page, buf, slot, sem):
    return pltpu.make_async_copy(cache_hbm.at[layer, page], buf.at[:, pl.ds(slot * PAGE_SIZE, PAGE_SIZE)], sem)


def _gather_pages(pt_ref, b, cache_hbm, layer, buf, sem, n_pages):
    def start(p, carry):
        _page_copy(cache_hbm, layer, pt_ref[b, p], buf, p, sem).start()
        return carry
    lax.fori_loop(0, n_pages, start, 0)


def _wait_pages(cache_hbm, layer, buf, sem, n_pages):
    def wait(p, carry):
        _page_copy(cache_hbm, layer, 0, buf, p, sem).wait()
        return carry
    lax.fori_loop(0, n_pages, wait, 0)


def _compress_paged_kernel(pt_ref, kc_hbm, vc_hbm, pe_ref, wf_ref, ws_ref, wr_ref, w2_ref, ck_ref, cv_ref,
                           kbuf, vbuf, rows_scr, sems, *, layer, n_pages):
    b = pl.program_id(0)
    n_units = n_pages * (PAGE_SIZE // CMP_STRIDE)
    _gather_pages(pt_ref, b, kc_hbm, layer, kbuf, sems.at[0], n_pages)
    _gather_pages(pt_ref, b, vc_hbm, layer, vbuf, sems.at[1], n_pages)
    for idx, (src_hbm, buf, dst) in enumerate(((kc_hbm, kbuf, ck_ref), (vc_hbm, vbuf, cv_ref))):
        _wait_pages(src_hbm, layer, buf, sems.at[idx], n_pages)

        def to_rows(c, carry, buf=buf):
            for p in range(pages_per_trip):
                off = pl.multiple_of((c * pages_per_trip + p) * PAGE_SIZE, PAGE_SIZE)
                rows_scr[pl.ds(off, PAGE_SIZE), :] = buf[:, pl.ds(off, PAGE_SIZE)].T
            return carry
        pages_per_trip = math.gcd(n_pages, 8)
        lax.fori_loop(0, n_pages // pages_per_trip, to_rows, 0)

        units = jnp.concatenate([rows_scr[pl.ds(r, n_units, stride=CMP_STRIDE), :].astype(BF16)
                                 for r in range(CMP_STRIDE)], axis=1)
        acc = _dot(units, wr_ref[idx])
        pe_term = (_dot(jnp.broadcast_to(pe_ref[idx, 0:1, :], (SUBLANES, pe_ref.shape[2])).astype(BF16), wf_ref[idx])
                   + _dot(jnp.broadcast_to(pe_ref[idx, 1:2, :], (SUBLANES, pe_ref.shape[2])).astype(BF16), ws_ref[idx]))
        pre = acc[:, 0:KV_WIDTH] + pltpu.roll(acc[:, KV_WIDTH:2 * KV_WIDTH], n_units - 1, 0) + pe_term[0:1, :]
        dst[0] = _dot(_silu(pre).astype(BF16), w2_ref[idx]).astype(BF16)


def _compress_paged(page_table, cache_kt, cache_vt, lw, *, layer):
    batch, n_pages = page_table.shape
    past = n_pages * PAGE_SIZE
    n_units = past // CMP_STRIDE
    consts = [lw["cmp_pe"], lw["cmp_wf"], lw["cmp_ws"], lw["cmp_wr"], lw["cmp_w2"]]
    const_spec = lambda a: pl.BlockSpec(a.shape, lambda b, pt: (0,) * a.ndim)
    any_spec = pl.BlockSpec(memory_space=pl.ANY)
    out_spec = pl.BlockSpec((1, n_units, KV_WIDTH), lambda b, pt: (b, 0, 0))
    grid_spec = pltpu.PrefetchScalarGridSpec(
        num_scalar_prefetch=1, grid=(batch,),
        in_specs=[any_spec, any_spec] + [const_spec(a) for a in consts],
        out_specs=[out_spec, out_spec],
        scratch_shapes=[pltpu.VMEM((KV_WIDTH, past), F32), pltpu.VMEM((KV_WIDTH, past), F32),
                        pltpu.VMEM((past, KV_WIDTH), F32), pltpu.SemaphoreType.DMA((2,))])
    return pl.pallas_call(
        functools.partial(_compress_paged_kernel, layer=layer, n_pages=n_pages),
        grid_spec=grid_spec,
        out_shape=[jax.ShapeDtypeStruct((batch, n_units, KV_WIDTH), BF16)] * 2,
        compiler_params=pltpu.CompilerParams(dimension_semantics=("arbitrary",), vmem_limit_bytes=VMEM_LIMIT_BYTES),
        name="compress_paged",
    )(page_table, cache_kt, cache_vt, *consts)


def _stack_heads(q):
    lane = _iota((q.shape[0], LANES), 1)
    lo = lane < HEAD_DIM
    c0, c1, c2 = (q[:, c * LANES:(c + 1) * LANES] for c in range(3))
    blocks = [jnp.where(lo, c0, 0.0), jnp.where(lo, pltpu.roll(c0, HEAD_DIM, 1), 0.0), jnp.where(lo, c1, 0.0),
              jnp.where(lo, 0.0, c1), jnp.where(lo, 0.0, pltpu.roll(c2, HEAD_DIM, 1)), jnp.where(lo, 0.0, c2)]
    return jnp.concatenate(blocks, axis=0)


def _unstack_heads(o, rows):
    lane = _iota((rows, LANES), 1)
    lo = lane < HEAD_DIM
    blk = [o[j * rows:(j + 1) * rows] for j in range(ATT_HEADS)]
    return jnp.concatenate([jnp.where(lo, blk[0], pltpu.roll(blk[1], HEAD_DIM, 1)),
                            jnp.where(lo, blk[2], blk[3]),
                            jnp.where(lo, pltpu.roll(blk[4], HEAD_DIM, 1), blk[5])], axis=1)


def _tile_heads(x, reps):
    return jnp.concatenate([x] * reps, axis=0)


def _masked_softmax(s, mask):
    s = jnp.where(mask, s, NEG)
    e = jnp.where(mask, jnp.exp(s - jnp.max(s, axis=1, keepdims=True)), 0.0)
    l = jnp.sum(e, axis=1, keepdims=True)
    return e * (1.0 / jnp.where(l > 0.0, l, 1.0))


def _split_bf16(x):
    hi = x.astype(BF16)
    return hi, (x - hi.astype(F32)).astype(BF16)


def _select_blocks(imp, tq, axis):
    n_blocks = imp.shape[axis]
    blk = _iota(imp.shape, axis)
    cur = tq >> 6
    causal = blk * SEL_BLOCK <= tq
    forced = causal & ((blk == 0) | (blk == cur) | (blk == cur - 1))
    score = jnp.where(forced, FORCE, jnp.where(causal, imp, -FORCE))
    blk_f = blk.astype(F32)
    sel = jnp.zeros(imp.shape, F32)
    for _ in range(N_SEL):
        mx = jnp.max(score, axis=axis, keepdims=True)
        first = jnp.min(jnp.where(score == mx, blk_f, float(n_blocks)), axis=axis, keepdims=True)
        pick = blk_f == first
        sel = jnp.where(pick & (mx > -FORCE / 2), 1.0, sel)
        score = jnp.where(pick, -jnp.inf, score)
    return sel


def _combine_gates(gates, o_c, o_s, o_w, rows):
    out = []
    for hh in range(ATT_HEADS):
        sl = slice(hh * rows, (hh + 1) * rows)
        g0 = GATE_LANE0 + 3 * hh
        out.append(gates[:, g0:g0 + 1] * o_c[sl] + gates[:, g0 + 1:g0 + 2] * o_s[sl] + gates[:, g0 + 2:g0 + 3] * o_w[sl])
    return jnp.concatenate(out, axis=0)


def _attn_prompt_kernel(qr_ref, qn_ref, small_ref, ck_ref, cvt_ref, ks_ref, vs_ref, kw_ref, vw_ref,
                        onehot_ref, mimp_ref, out_ref,
                        qaug_scr, sa_scr, sb_scr, p_scr, m_scr, l_scr, a_scr, acc_scr,
                        sct_scr, pct_scr, psum_scr, sw_scr, pw_scr, lw_scr):
    R = Q_BLOCK
    TK = KEYS_PER_TILE
    i = pl.program_id(1)
    t0 = pl.multiple_of(i * R, R)
    scale = HEAD_DIM ** -0.5
    tq = t0 + _iota((R, 1), 0)
    tq_row = t0 + _iota((1, R), 1)
    q_rope = _stack_heads(qr_ref[...] * scale).astype(BF16)
    q_raw = _stack_heads(qn_ref[...] * scale).astype(BF16)
    qaug_scr[:, 0:LANES] = q_rope

    n_units = ck_ref.shape[1]
    sct_scr[...] = _dot_nt(ck_ref[0], q_raw)
    cmp_ok = (_iota((n_units, 1), 0) * CMP_STRIDE + (CMP_LEN - 1)) <= tq_row
    for j in range(ATT_HEADS):
        cols = slice(j * R, (j + 1) * R)
        s = jnp.where(cmp_ok, sct_scr[:, cols], NEG)
        e = jnp.where(cmp_ok, jnp.exp(s - jnp.max(s, axis=0, keepdims=True)), 0.0)
        l = jnp.sum(e, axis=0, keepdims=True)
        p = e * (1.0 / jnp.where(l > 0.0, l, 1.0))
        pct_scr[:, cols] = p.astype(BF16)
        k, g = divmod(j, GQA)
        if g == 0:
            psum_scr[k] = p
        else:
            psum_scr[k] += p
    o_c_t = _dot(cvt_ref[0], pct_scr[...])
    o_c = jnp.concatenate([o_c_t[:, j * R:(j + 1) * R].T for j in range(ATT_HEADS)], axis=0)

    for k in range(ATT_KV_HEADS):
        hi, lo = _split_bf16(psum_scr[k])
        imp_t = _dot(mimp_ref[...], hi) + _dot(mimp_ref[...], lo)
        sel_t = _select_blocks(imp_t, tq_row, 0)
        bias = jnp.where(sel_t.T > 0.5, 0.0, NEG).astype(BF16)
        for g in range(GQA):
            j = k * GQA + g
            qaug_scr[j * R:(j + 1) * R, LANES:2 * LANES] = bias

    def scores(k0, s_ref):
        k_aug = jnp.concatenate([ks_ref[pl.ds(k0, TK), :], onehot_ref[pl.ds(k0, TK), :]], axis=1)
        s_ref[...] = _dot_nt(qaug_scr[...], k_aug)

    def accumulate(k0, s_ref, first):
        causal = (k0 + _iota((1, TK), 1)) <= tq
        for j in range(ATT_HEADS):
            rows = slice(j * R, (j + 1) * R)
            s = s_ref[rows, :]
            if first:
                s = jnp.where(causal, s, NEG)
                m_new = jnp.broadcast_to(jnp.max(s, axis=1, keepdims=True), (R, LANES))
            else:
                m_old = m_scr[rows, :]
                m_new = jnp.maximum(m_old, jnp.max(s, axis=1, keepdims=True))
                alpha = jnp.exp(m_old - m_new)
            p = jnp.exp(s - jnp.concatenate([m_new] * (TK // LANES), axis=1))
            l_new = jnp.sum(p, axis=1, keepdims=True)
            p_scr[rows, :] = p.astype(BF16)
            m_scr[rows, :] = m_new
            if first:
                l_scr[rows, :] = jnp.broadcast_to(l_new, (R, LANES))
            else:
                l_scr[rows, :] = alpha * l_scr[rows, :] + l_new
                a_scr[rows, :] = alpha
        pv = _dot(p_scr[...], vs_ref[pl.ds(k0, TK), :])
        if first:
            acc_scr[...] = pv
        else:
            acc_scr[...] = a_scr[...] * acc_scr[...] + pv

    n_past = t0 // TK
    k_diag = pl.multiple_of(n_past * TK, TK)
    scores(k_diag, sa_scr)
    accumulate(k_diag, sa_scr, True)

    @pl.when(n_past % 2 == 1)
    def _():
        k_odd = pl.multiple_of((n_past - 1) * TK, TK)
        scores(k_odd, sb_scr)
        accumulate(k_odd, sb_scr, False)

    scores(0, sa_scr)

    def pair_body(jj, carry):
        k_even = pl.multiple_of(2 * jj * TK, TK)
        k_odd = pl.multiple_of(k_even + TK, TK)
        scores(k_odd, sb_scr)
        accumulate(k_even, sa_scr, False)
        scores(pl.multiple_of(k_odd + TK, TK), sa_scr)
        accumulate(k_odd, sb_scr, False)
        return carry
    lax.fori_loop(0, n_past // 2, pair_body, 0)
    o_s = acc_scr[...] * (1.0 / l_scr[...])

    span = WINDOW + R
    s0 = pl.multiple_of(jnp.maximum(t0 - WINDOW, 0), R)
    sw_scr[...] = _dot_nt(q_rope, kw_ref[pl.ds(s0, span), :])
    spos = s0 + _iota((1, span), 1)
    win_ok = (spos <= tq) & (spos > tq - WINDOW)
    for j in range(ATT_HEADS):
        rows = slice(j * R, (j + 1) * R)
        s = jnp.where(win_ok, sw_scr[rows, :], NEG)
        e = jnp.where(win_ok, jnp.exp(s - jnp.max(s, axis=1, keepdims=True)), 0.0)
        pw_scr[rows, :] = e.astype(BF16)
        lw_scr[rows, :] = jnp.broadcast_to(jnp.sum(e, axis=1, keepdims=True), (R, LANES))
    o_w = _dot(pw_scr[...], vw_ref[pl.ds(s0, span), :]) * (1.0 / lw_scr[...])

    out_ref[...] = _unstack_heads(_combine_gates(small_ref[...], o_c, o_s, o_w, R), R)


def _attn_prompt(qrope, qraw, small, ck, cvt, kselb, vselb, kwinb, vwinb, *, batch, seq_len):
    span = WINDOW + Q_BLOCK
    assert seq_len % KEYS_PER_TILE == 0 and seq_len >= span and seq_len // SEL_BLOCK <= LANES
    nqb = seq_len // Q_BLOCK
    n_units = seq_len // CMP_STRIDE
    t = batch * seq_len
    pos = jnp.arange(seq_len, dtype=jnp.int32)
    onehot = (pos[:, None] // SEL_BLOCK == jnp.arange(LANES, dtype=jnp.int32)[None, :]).astype(BF16)
    m_imp_t = _importance_matrix(n_units, LANES).T
    q_spec = lambda w: pl.BlockSpec((Q_BLOCK, w), lambda b, i: (b * nqb + i, 0))
    batch_spec = lambda a: pl.BlockSpec((1,) + a.shape[1:], lambda b, i: (b, 0, 0))
    seq_spec = pl.BlockSpec((seq_len, KV_WIDTH), lambda b, i: (b, 0))
    const_spec = lambda a: pl.BlockSpec(a.shape, lambda b, i: (0, 0))
    rows = ATT_HEADS * Q_BLOCK
    vmem = pltpu.VMEM
    scratch = [vmem((rows, 2 * LANES), BF16),
               vmem((rows, KEYS_PER_TILE), F32), vmem((rows, KEYS_PER_TILE), F32),
               vmem((rows, KEYS_PER_TILE), BF16),
               vmem((rows, LANES), F32), vmem((rows, LANES), F32), vmem((rows, LANES), F32),
               vmem((rows, LANES), F32),
               vmem((n_units, rows), F32), vmem((n_units, rows), BF16),
               vmem((ATT_KV_HEADS, n_units, Q_BLOCK), F32),
               vmem((rows, span), F32), vmem((rows, span), BF16), vmem((rows, LANES), F32)]
    return pl.pallas_call(
        _attn_prompt_kernel,
        grid=(batch, nqb),
        in_specs=[q_spec(ATT_WIDTH), q_spec(ATT_WIDTH), q_spec(LANES), batch_spec(ck), batch_spec(cvt),
                  seq_spec, seq_spec, seq_spec, seq_spec, const_spec(onehot), const_spec(m_imp_t)],
        out_specs=q_spec(ATT_WIDTH),
        out_shape=jax.ShapeDtypeStruct((t, ATT_WIDTH), F32),
        scratch_shapes=scratch,
        compiler_params=pltpu.CompilerParams(dimension_semantics=("arbitrary", "arbitrary"),
                                             vmem_limit_bytes=VMEM_LIMIT_BYTES),
        name="attn_prompt",
    )(qrope, qraw, small, ck, cvt, kselb, vselb, kwinb, vwinb, onehot, m_imp_t)


def _importance_matrix(n_units, width):
    n = jnp.arange(n_units, dtype=jnp.int32)[:, None]
    j = jnp.arange(width, dtype=jnp.int32)[None, :]
    upb = SEL_BLOCK // CMP_STRIDE
    m = (n // upb == j).astype(F32) + ((n + 1) // upb == j).astype(F32)
    return jnp.where(n < n_units - 1, m, 0.0).astype(BF16)


def _attn_sample_kernel(pt_ref, ks_hbm, vs_hbm, qr_ref, qn_ref, small_ref, ck_ref, cv_ref,
                        knew_ref, vnew_ref, kwc_ref, vwc_ref, kwnew_ref, vwnew_ref, onehot_ref, mimp_ref,
                        out_ref, kbuf, vbuf, sems, *, layer, n_pages, past, win_buf, group_blocks):
    b = pl.program_id(0)
    R = qr_ref.shape[0]
    _gather_pages(pt_ref, b, ks_hbm, layer, kbuf, sems.at[0], n_pages)
    _gather_pages(pt_ref, b, vs_hbm, layer, vbuf, sems.at[1], n_pages)

    scale = HEAD_DIM ** -0.5
    tq = past + _iota((R, 1), 0)
    tq_all = _tile_heads(tq, ATT_HEADS)
    q_rope = _stack_heads(qr_ref[...] * scale).astype(BF16)
    q_raw = _stack_heads(qn_ref[...] * scale).astype(BF16)

    n_units = ck_ref.shape[1]
    s_c = _dot_nt(q_raw, ck_ref[0])
    cmp_end = _iota((1, n_units), 1) * CMP_STRIDE + (CMP_LEN - 1)
    p_c = _masked_softmax(s_c, cmp_end <= tq_all)
    o_c = _dot(p_c.astype(BF16), cv_ref[0])

    bias = []
    for k in range(ATT_KV_HEADS):
        hi, lo = _split_bf16(sum(p_c[(k * GQA + g) * R:(k * GQA + g + 1) * R] for g in range(GQA)))
        sel = _select_blocks(_dot(hi, mimp_ref[...]) + _dot(lo, mimp_ref[...]), tq, 1)
        bias.append(_tile_heads(jnp.where(sel > 0.5, 0.0, NEG), GQA))
    bias = jnp.concatenate(bias, axis=0).astype(BF16)

    k_off = past - win_buf
    s_wc = _dot(q_rope, kwc_ref[0, 0].astype(BF16))
    s_wn = _dot_nt(q_rope, kwnew_ref[...])
    pos_c = k_off + _iota((1, win_buf), 1)
    pos_n = past + _iota((1, R), 1)
    mask_c = (pos_c <= tq_all) & (pos_c > tq_all - WINDOW) & (pos_c >= k_off)
    mask_n = (pos_n <= tq_all) & (pos_n > tq_all - WINDOW) & (pos_n >= k_off)
    s_wc = jnp.where(mask_c, s_wc, NEG)
    s_wn = jnp.where(mask_n, s_wn, NEG)
    m_w = jnp.maximum(jnp.max(s_wc, axis=1, keepdims=True), jnp.max(s_wn, axis=1, keepdims=True))
    e_wc = jnp.where(mask_c, jnp.exp(s_wc - m_w), 0.0)
    e_wn = jnp.where(mask_n, jnp.exp(s_wn - m_w), 0.0)
    l_w = jnp.sum(e_wc, axis=1, keepdims=True) + jnp.sum(e_wn, axis=1, keepdims=True)
    o_w = _dot_nt(e_wc.astype(BF16), vwc_ref[0, 0].astype(BF16)) + _dot(e_wn.astype(BF16), vwnew_ref[...])
    o_w = o_w * (1.0 / jnp.where(l_w > 0.0, l_w, 1.0))

    s_n = _dot_nt(q_rope, knew_ref[...])
    mask_sn = pos_n <= tq_all
    s_n = jnp.where(mask_sn, s_n, NEG)
    _wait_pages(ks_hbm, layer, kbuf, sems.at[0], n_pages)
    group_keys = group_blocks * SEL_BLOCK
    n_groups = (n_pages * PAGE_SIZE) // group_keys
    s_past = []
    for gi in range(n_groups):
        keys = slice(gi * group_keys, (gi + 1) * group_keys)
        q_aug = jnp.concatenate([q_rope, bias[:, gi * group_blocks:(gi + 1) * group_blocks]], axis=1)
        k_aug_t = jnp.concatenate([kbuf[:, keys].astype(BF16), onehot_ref[...]], axis=0)
        s_past.append(_dot(q_aug, k_aug_t))
    m_s = jnp.max(s_n, axis=1, keepdims=True)
    for s in s_past:
        m_s = jnp.maximum(m_s, jnp.max(s, axis=1, keepdims=True))
    e_n = jnp.where(mask_sn, jnp.exp(s_n - m_s), 0.0)
    l_s = jnp.sum(e_n, axis=1, keepdims=True)
    o_s = _dot(e_n.astype(BF16), vnew_ref[...])
    _wait_pages(vs_hbm, layer, vbuf, sems.at[1], n_pages)
    for gi, s in enumerate(s_past):
        e = jnp.exp(s - m_s)
        l_s = l_s + jnp.sum(e, axis=1, keepdims=True)
        o_s = o_s + _dot_nt(e.astype(BF16), vbuf[:, gi * group_keys:(gi + 1) * group_keys].astype(BF16))
    o_s = o_s * (1.0 / l_s)

    out_ref[...] = _unstack_heads(_combine_gates(small_ref[...], o_c, o_s, o_w, R), R)


def _attn_sample(page_table, cache_kst, cache_vst, cache_kwt, cache_vwt, qrope, qraw, small, ck, cv,
                 knew, vnew, kwnew, vwnew, *, layer, seq_len):
    batch, n_pages = page_table.shape
    past = n_pages * PAGE_SIZE
    win_buf = cache_kwt.shape[3]
    assert seq_len == SUBLANES and (past + seq_len) // CMP_STRIDE == past // CMP_STRIDE and past >= WINDOW
    n_units = past // CMP_STRIDE
    nb_past = past // SEL_BLOCK
    group_blocks = min(LANES, nb_past)
    assert nb_past % group_blocks == 0
    n_blk_lanes = -(-(nb_past + 1) // LANES) * LANES
    key = jnp.arange(group_blocks * SEL_BLOCK, dtype=jnp.int32)
    onehot = (key[None, :] // SEL_BLOCK == jnp.arange(group_blocks, dtype=jnp.int32)[:, None]).astype(BF16)
    m_imp = _importance_matrix(n_units, n_blk_lanes)
    win_spec = pl.BlockSpec((1, 1, KV_WIDTH, win_buf), lambda b, pt: (layer, b, 0, 0))
    any_spec = pl.BlockSpec(memory_space=pl.ANY)
    row_spec = lambda w: pl.BlockSpec((seq_len, w), lambda b, pt: (b, 0))
    batch_spec = lambda a: pl.BlockSpec((1,) + a.shape[1:], lambda b, pt: (b, 0, 0))
    const_spec = lambda a: pl.BlockSpec(a.shape, lambda b, pt: (0, 0))
    grid_spec = pltpu.PrefetchScalarGridSpec(
        num_scalar_prefetch=1, grid=(batch,),
        in_specs=[any_spec, any_spec, row_spec(ATT_WIDTH), row_spec(ATT_WIDTH), row_spec(LANES),
                  batch_spec(ck), batch_spec(cv), row_spec(KV_WIDTH), row_spec(KV_WIDTH),
                  win_spec, win_spec, row_spec(KV_WIDTH), row_spec(KV_WIDTH),
                  const_spec(onehot), const_spec(m_imp)],
        out_specs=row_spec(ATT_WIDTH),
        scratch_shapes=[pltpu.VMEM((KV_WIDTH, past), F32), pltpu.VMEM((KV_WIDTH, past), F32),
                        pltpu.SemaphoreType.DMA((2,))])
    return pl.pallas_call(
        functools.partial(_attn_sample_kernel, layer=layer, n_pages=n_pages, past=past, win_buf=win_buf,
                          group_blocks=group_blocks),
        grid_spec=grid_spec,
        out_shape=jax.ShapeDtypeStruct((batch * seq_len, ATT_WIDTH), F32),
        compiler_params=pltpu.CompilerParams(dimension_semantics=("arbitrary",), vmem_limit_bytes=VMEM_LIMIT_BYTES),
        name="attn_sample",
    )(page_table, cache_kst, cache_vst, qrope, qraw, small, ck, cv, knew, vnew, cache_kwt, cache_vwt, kwnew, vwnew,
      onehot, m_imp)


def _ffn_kernel(h_ref, yssd_ref, ysc_ref, yatt_ref, wo_ref, n2_ref, w1_ref, w2_ref, fn_ref, out_ref,
                xn_scr, hs_scr, acc_scr, *, n_ff, final):
    j = pl.program_id(1)

    @pl.when(j == 0)
    def _():
        mix = (_dot(yssd_ref[...].astype(BF16), wo_ref[0:SSD_WIDTH, :])
               + _dot(ysc_ref[...].astype(BF16), wo_ref[SSD_WIDTH:SSD_WIDTH + SC_WIDTH, :])
               + _dot(yatt_ref[...].astype(BF16), wo_ref[SSD_WIDTH + SC_WIDTH:D_MODEL, :]))
        hn = h_ref[...] + mix
        hs_scr[...] = hn
        xn_scr[...] = _rms(hn, n2_ref[...]).astype(BF16)
        acc_scr[...] = jnp.zeros_like(acc_scr)

    hid = jnp.square(jnp.maximum(_dot(xn_scr[...], w1_ref[...]), 0.0))
    acc_scr[...] += _dot(hid.astype(BF16), w2_ref[...])

    @pl.when(j == n_ff - 1)
    def _():
        o = hs_scr[...] + acc_scr[...]
        out_ref[...] = _rms(o, fn_ref[...]) if final else o


def _ffn(h, yssd, ysc, yatt, lw, final_norm_w, *, final):
    t = h.shape[0]
    tm = min(FFN_ROWS, t)
    tf = FFN_HIDDEN_TILE
    assert t % tm == 0 and D_FF % tf == 0
    n_ff = D_FF // tf
    row_spec = lambda w: pl.BlockSpec((tm, w), lambda i, j: (i, 0))
    const_spec = lambda a: pl.BlockSpec(a.shape, lambda i, j: (0, 0))
    return pl.pallas_call(
        functools.partial(_ffn_kernel, n_ff=n_ff, final=final),
        grid=(t // tm, n_ff),
        in_specs=[row_spec(D_MODEL), row_spec(SSD_WIDTH), row_spec(SC_WIDTH), row_spec(ATT_WIDTH),
                  const_spec(lw["w_out"]), const_spec(lw["norm2_w"]),
                  pl.BlockSpec((D_MODEL, tf), lambda i, j: (0, j)), pl.BlockSpec((tf, D_MODEL), lambda i, j: (j, 0)),
                  const_spec(final_norm_w)],
        out_specs=row_spec(D_MODEL),
        out_shape=jax.ShapeDtypeStruct((t, D_MODEL), F32),
        scratch_shapes=[pltpu.VMEM((tm, D_MODEL), BF16), pltpu.VMEM((tm, D_MODEL), F32), pltpu.VMEM((tm, D_MODEL), F32)],
        compiler_params=pltpu.CompilerParams(dimension_semantics=("arbitrary", "arbitrary"),
                                             vmem_limit_bytes=VMEM_LIMIT_BYTES),
        name="ffn",
    )(h, yssd, ysc, yatt, lw["w_out"], lw["norm2_w"], lw["w_ff1"], lw["w_ff2"], final_norm_w)


def _lane_row(v, width):
    v = v.reshape(1, -1).astype(F32)
    return jnp.pad(v, ((0, 0), (0, width - v.shape[1])))


def _prep_layer(l, norm1_w, w_in, ssd_conv_w, ssd_conv_b, ssd_dt_bias, ssd_a_log, ssd_d, ssd_norm_w, sc_conv_w,
                cmp_pe, cmp_w1, cmp_w2, w_out, norm2_w, w_ff1, w_ff2):
    offs = [0]
    for s in IN_SIZES:
        offs.append(offs[-1] + s)
    seg = lambda a: jnp.arange(offs[a], offs[a + 1])
    perm = jnp.concatenate([seg(0), seg(1), seg(3), seg(4), seg(5), seg(6)] + [seg(a) for a in range(7, 13)]
                           + [seg(2), seg(13)])
    w_perm = jnp.pad(w_in[l][:, perm], ((0, 0), (0, IN_PAD - offs[-1]))).astype(BF16)

    eye = jnp.eye(ATT_KV_HEADS, dtype=F32)
    w1 = cmp_w1[l].reshape(2, 2, CMP_STRIDE, HEAD_DIM, HEAD_DIM)
    wfs = jnp.einsum('vhrde,kK->vhrkdKe', w1, eye).reshape(2, 2, CMP_STRIDE * KV_WIDTH, KV_WIDTH).astype(BF16)
    pe = cmp_pe[l].reshape(2, 2, CMP_STRIDE, 1, HEAD_DIM)
    pe = jnp.broadcast_to(pe, (2, 2, CMP_STRIDE, ATT_KV_HEADS, HEAD_DIM)).reshape(2, 2, CMP_STRIDE * KV_WIDTH)
    w2 = jnp.einsum('vde,kK->vkdKe', cmp_w2[l], eye).reshape(2, KV_WIDTH, KV_WIDTH).astype(BF16)

    head = jnp.arange(LANES, dtype=jnp.int32)[:, None]
    col = jnp.arange(SSD_WIDTH, dtype=jnp.int32)[None, :]
    idx = jnp.arange(SSD_CHUNK, dtype=jnp.int32)
    return {
        "norm1_w": norm1_w[l].reshape(1, D_MODEL), "w_in": w_perm,
        "ssd_conv_w": ssd_conv_w[l], "ssd_conv_b": ssd_conv_b[l].reshape(1, SSD_XBC),
        "dt_bias": _lane_row(ssd_dt_bias[l], LANES), "a_log": _lane_row(ssd_a_log[l], LANES),
        "d_skip": jnp.repeat(ssd_d[l], HEAD_DIM).reshape(1, SSD_WIDTH),
        "ssd_norm_w": ssd_norm_w[l].reshape(1, SSD_WIDTH), "sc_conv_w": sc_conv_w[l],
        "tril": (idx[:, None] >= idx[None, :]).astype(F32),
        "expand": (col // HEAD_DIM == head).astype(F32),
        "cmp_pe": pe, "cmp_wf": wfs[:, 0], "cmp_ws": wfs[:, 1], "cmp_w2": w2,
        "cmp_wr": jnp.concatenate([wfs[:, 0], wfs[:, 1]], axis=-1), "cmp_w2t": w2.transpose(0, 2, 1),
        "w_out": w_out[l].astype(BF16), "norm2_w": norm2_w[l].reshape(1, D_MODEL),
        "w_ff1": w_ff1[l].astype(BF16), "w_ff2": w_ff2[l].astype(BF16),
    }


def _front_pad_rows(a, rows):
    return jnp.pad(a, ((0, 0), (rows - a.shape[1], 0), (0, 0)))


def _run_layer(h, lw, invf, ssm0, cssd0, csc0, attn_fn, final_norm_w, *, batch, seq_len, pos0, final):
    (z, xbc, sc, qraw, qrope, kcmp, vcmp, ksel, vsel, kwin, vwin, small, kcmp_rows, vcmp_rows,
     kselb, vselb, kwinb, vwinb) = _inproj(h, lw["norm1_w"], lw["w_in"], invf, batch=batch, seq_len=seq_len, pos0=pos0)
    yssd, ysc, ssm_new, cssd_new, csc_new = _mixer(
        z, xbc, sc, small, ssm0, _front_pad_rows(cssd0, SUBLANES), _front_pad_rows(csc0, SUBLANES), lw,
        batch=batch, seq_len=seq_len)
    rows = dict(kcmp=kcmp, vcmp=vcmp, ksel=ksel, vsel=vsel, kwin=kwin, vwin=vwin, kcmp_rows=kcmp_rows,
                vcmp_rows=vcmp_rows, kselb=kselb, vselb=vselb, kwinb=kwinb, vwinb=vwinb)
    yatt = attn_fn(qrope, qraw, small, rows)
    h = _ffn(h, yssd, ysc, yatt, lw, final_norm_w, final=final)
    states = (ssm_new, cssd_new[:, SUBLANES - (SSD_CONV - 1):], csc_new[:, SUBLANES - (SC_CONV - 1):])
    return h, states, rows


def kernel(x_prompt, x_sample, cache_k_cmp, cache_v_cmp, cache_k_sel, cache_v_sel, cache_k_win, cache_v_win, state_ssm, state_ssd_conv, state_sc_conv, page_table, norm1_w, w_in, ssd_conv_w, ssd_conv_b, ssd_dt_bias, ssd_a_log, ssd_d, ssd_norm_w, sc_conv_w, cmp_pe, cmp_w1, cmp_w2, w_out, norm2_w, w_ff1, w_ff2, final_norm_w):
    bp, lp, _ = x_prompt.shape
    db, ls, _ = x_sample.shape
    depth = w_in.shape[0]
    n_pages = page_table.shape[1]
    past = n_pages * PAGE_SIZE
    win_buf = cache_k_win.shape[2]
    fnw = final_norm_w.reshape(1, D_MODEL)

    half = ROT_DIM // 2
    inv_freq = ROPE_THETA ** (-jnp.arange(half, dtype=F32) * 2.0 / ROT_DIM)
    invf = jnp.tile(inv_freq, LANES // half).reshape(1, LANES)

    hp = x_prompt.reshape(bp * lp, D_MODEL)
    hs = x_sample.reshape(db * ls, D_MODEL)
    ssm0_p = jnp.zeros((bp, SSD_HEADS, HEAD_DIM, SSD_STATE), F32)
    cssd0_p = jnp.zeros((bp, SSD_CONV - 1, SSD_XBC), F32)
    csc0_p = jnp.zeros((bp, SC_CONV - 1, SC_WIDTH), F32)
    p_lists = [[] for _ in range(9)]
    s_lists = [[] for _ in range(9)]
    keep_p = min(WINDOW, lp)
    keep_s = min(WINDOW, win_buf + ls)

    def kv4(a, b, n):
        if a.ndim == 3:
            return a.reshape(b, ATT_KV_HEADS, HEAD_DIM, n).transpose(0, 3, 1, 2)
        return a.reshape(b, n, ATT_KV_HEADS, HEAD_DIM)

    def dims_major(c):
        return c.transpose(0, 1, 3, 4, 2).reshape(c.shape[0], c.shape[1], KV_WIDTH, c.shape[2])

    pages_kc, pages_vc, pages_ks, pages_vs = (dims_major(c) for c in (cache_k_cmp, cache_v_cmp, cache_k_sel, cache_v_sel))
    win_k, win_v = dims_major(cache_k_win), dims_major(cache_v_win)

    for l in range(depth):
        lw = _prep_layer(l, norm1_w, w_in, ssd_conv_w, ssd_conv_b, ssd_dt_bias, ssd_a_log, ssd_d, ssd_norm_w,
                         sc_conv_w, cmp_pe, cmp_w1, cmp_w2, w_out, norm2_w, w_ff1, w_ff2)
        final = l == depth - 1

        def prompt_attn(qrope, qraw, small, rows, lw=lw):
            ck, cvt = _compress_prompt(rows["kcmp_rows"], rows["vcmp_rows"], lw, batch=bp, seq_len=lp)
            return _attn_prompt(qrope, qraw, small, ck, cvt, rows["kselb"], rows["vselb"], rows["kwinb"], rows["vwinb"],
                                batch=bp, seq_len=lp)

        def sample_attn(qrope, qraw, small, rows, lw=lw, l=l):
            ck, cv = _compress_paged(page_table, pages_kc, pages_vc, lw, layer=l)
            return _attn_sample(page_table, pages_ks, pages_vs, win_k, win_v, qrope, qraw, small, ck, cv,
                                rows["kselb"], rows["vselb"], rows["kwinb"], rows["vwinb"], layer=l, seq_len=ls)

        hp, st_p, rows_p = _run_layer(hp, lw, invf, ssm0_p, cssd0_p, csc0_p, prompt_attn, fnw,
                                      batch=bp, seq_len=lp, pos0=0, final=final)
        hs, st_s, rows_s = _run_layer(hs, lw, invf, state_ssm[l], state_ssd_conv[l], state_sc_conv[l], sample_attn, fnw,
                                      batch=db, seq_len=ls, pos0=past, final=final)
        p_new = [kv4(rows_p[n], bp, lp) for n in ("kcmp", "vcmp", "ksel", "vsel")]
        p_new += [kv4(rows_p[n], bp, lp)[:, -keep_p:] for n in ("kwin", "vwin")]
        p_new += list(st_p)
        s_new = [kv4(rows_s[n], db, ls) for n in ("kcmp", "vcmp", "ksel", "vsel")]
        s_new += [jnp.concatenate([cache_k_win[l], kv4(rows_s["kwin"], db, ls)], axis=1)[:, -keep_s:],
                  jnp.concatenate([cache_v_win[l], kv4(rows_s["vwin"], db, ls)], axis=1)[:, -keep_s:]]
        s_new += list(st_s)
        for lst, arr in zip(p_lists, p_new):
            lst.append(arr)
        for lst, arr in zip(s_lists, s_new):
            lst.append(arr)

    p_out = [jnp.stack(a) for a in p_lists]
    s_out = [jnp.stack(a) for a in s_lists]
    y_prompt = hp.reshape(bp, lp, D_MODEL)
    y_sample = hs.reshape(db, ls, D_MODEL)
    return (y_prompt, y_sample, *p_out, *s_out)
```

```python
import functools
import math

import jax
import jax.numpy as jnp
from jax import lax
from jax.experimental import pallas as pl
from jax.experimental.pallas import tpu as pltpu

F32 = jnp.float32
BF16 = jnp.bfloat16
HIGHEST = lax.Precision.HIGHEST

D_MODEL = 1024
HEAD_DIM = 64
SSD_WIDTH = 384
SSD_HEADS = 6
SSD_GROUPS = 2
SSD_STATE = 128
SSD_CONV = 4
SSD_CHUNK = 128
SSD_XBC = SSD_WIDTH + 2 * SSD_GROUPS * SSD_STATE
SC_WIDTH = 256
SC_CONV = 3
ATT_WIDTH = 384
ATT_HEADS = 6
ATT_KV_HEADS = 2
GQA = ATT_HEADS // ATT_KV_HEADS
KV_WIDTH = ATT_KV_HEADS * HEAD_DIM
CMP_LEN = 32
CMP_STRIDE = 16
SEL_BLOCK = 64
N_SEL = 16
WINDOW = 512
Q_BLOCK = 128
PAGE_SIZE = 128
ROPE_THETA = 500000.0
ROT_DIM = HEAD_DIM // 4
D_FF = 4 * D_MODEL
EPS = 1e-6
NEG = -1e30
FORCE = 1e4
IN_SIZES = (SSD_WIDTH, SSD_XBC, SSD_HEADS, SC_WIDTH, SC_WIDTH, SC_WIDTH, ATT_WIDTH,
            KV_WIDTH, KV_WIDTH, KV_WIDTH, KV_WIDTH, KV_WIDTH, KV_WIDTH, ATT_HEADS * 3)

LANES = 128
SUBLANES = 8
VMEM_LIMIT_BYTES = 56 * 1024 * 1024

COL_Z = 0
COL_XBC = COL_Z + SSD_WIDTH
COL_SC = COL_XBC + SSD_XBC
COL_Q = COL_SC + 3 * SC_WIDTH
COL_KV = COL_Q + ATT_WIDTH
COL_SMALL = COL_KV + 6 * KV_WIDTH
IN_PAD = COL_SMALL + LANES
GATE_LANE0 = SSD_HEADS
KEYS_PER_TILE = 512
INPROJ_ROWS = 512
FFN_ROWS = 512


def _dot(a, b):
    return jnp.dot(a, b, preferred_element_type=F32)


def _dot_nt(a, b):
    return lax.dot_general(a, b, (((1,), (1,)), ((), ())), preferred_element_type=F32)


def _dot_exact(a, b):
    return jnp.dot(a, b, preferred_element_type=F32, precision=HIGHEST)


def _silu(x):
    return x * (1.0 / (1.0 + jnp.exp(-x)))


def _sigmoid(x):
    return 1.0 / (1.0 + jnp.exp(-x))


def _rms(x, w):
    return x * lax.rsqrt(jnp.mean(x * x, axis=-1, keepdims=True) + EPS) * w


def _iota(shape, dim):
    return lax.broadcasted_iota(jnp.int32, shape, dim)


def _rope_chunk(x, cos_t, sin_t, ll):
    fwd = pltpu.roll(x, LANES - ROT_DIM // 2, 1)
    bwd = pltpu.roll(x, ROT_DIM // 2, 1)
    partner = jnp.where(ll < ROT_DIM // 2, fwd, bwd)
    return x * cos_t + partner * sin_t


def _inproj_kernel(x_ref, nw_ref, w_ref, invf_ref,
                   z_ref, xbc_ref, sc_ref, qraw_ref, qrope_ref,
                   kcmp_ref, vcmp_ref, ksel_ref, vsel_ref, kwin_ref, vwin_ref, small_ref,
                   kcmp_rows_ref, vcmp_rows_ref, kselb_ref, vselb_ref, kwinb_ref, vwinb_ref,
                   y_scr, *, tm, seq_len, pos0, dims_major):
    i = pl.program_id(0)
    y_scr[...] = _dot(_rms(x_ref[...], nw_ref[...]).astype(BF16), w_ref[...])

    def proj(c0, width):
        return y_scr[:, c0:c0 + width]

    z_ref[...] = proj(COL_Z, SSD_WIDTH)
    xbc_ref[...] = proj(COL_XBC, SSD_XBC)
    sc_ref[...] = proj(COL_SC, 3 * SC_WIDTH)

    row = i * tm + _iota((tm, LANES), 0)
    pos = (pos0 + (row & (seq_len - 1))).astype(F32)
    lane = _iota((tm, LANES), 1)
    ll = lane & (HEAD_DIM - 1)
    ang = pos * invf_ref[...]
    cos_a = jnp.cos(ang)
    sin_a = jnp.sin(ang)
    half = ROT_DIM // 2
    cos_t = jnp.where(ll < ROT_DIM, cos_a, 1.0)
    sin_t = jnp.where(ll < half, -sin_a, jnp.where(ll < ROT_DIM, sin_a, 0.0))

    q = proj(COL_Q, ATT_WIDTH)
    qraw_ref[...] = q.astype(BF16)
    qrope_ref[...] = jnp.concatenate(
        [_rope_chunk(q[:, c * LANES:(c + 1) * LANES], cos_t, sin_t, ll) for c in range(ATT_WIDTH // LANES)],
        axis=1).astype(BF16)

    kv = proj(COL_KV, 6 * KV_WIDTH)
    kcmp = kv[:, 0:LANES]
    vcmp = kv[:, LANES:2 * LANES]
    ksel = _rope_chunk(kv[:, 2 * LANES:3 * LANES], cos_t, sin_t, ll)
    vsel = kv[:, 3 * LANES:4 * LANES]
    kwin = _rope_chunk(kv[:, 4 * LANES:5 * LANES], cos_t, sin_t, ll)
    vwin = kv[:, 5 * LANES:6 * LANES]
    for ref, val in ((kcmp_ref, kcmp), (vcmp_ref, vcmp), (ksel_ref, ksel), (vsel_ref, vsel),
                     (kwin_ref, kwin), (vwin_ref, vwin)):
        if dims_major:
            ref[0] = val.T
        else:
            ref[...] = val
    kcmp_rows_ref[...] = kcmp
    vcmp_rows_ref[...] = vcmp
    kselb_ref[...] = ksel.astype(BF16)
    vselb_ref[...] = vsel.astype(BF16)
    kwinb_ref[...] = kwin.astype(BF16)
    vwinb_ref[...] = vwin.astype(BF16)

    sm = proj(COL_SMALL, LANES)
    is_gate = (lane >= GATE_LANE0) & (lane < GATE_LANE0 + 3 * ATT_HEADS)
    small_ref[...] = jnp.where(is_gate, _sigmoid(sm), sm)


def _inproj(h, norm_w, w_perm, invf, *, batch, seq_len, pos0):
    t = h.shape[0]
    tm = min(INPROJ_ROWS, t)
    assert t % tm == 0 and seq_len & (seq_len - 1) == 0
    dims_major = seq_len % tm == 0
    tiles_per_row = max(seq_len // tm, 1)
    row_spec = lambda w: pl.BlockSpec((tm, w), lambda i: (i, 0))
    const_spec = lambda a: pl.BlockSpec(a.shape, lambda i: (0, 0))
    row_out = lambda w, dt: (row_spec(w), jax.ShapeDtypeStruct((t, w), dt))
    if dims_major:
        kv_out = (pl.BlockSpec((1, KV_WIDTH, tm), lambda i: (i // tiles_per_row, 0, i % tiles_per_row)),
                  jax.ShapeDtypeStruct((batch, KV_WIDTH, seq_len), F32))
    else:
        kv_out = row_out(KV_WIDTH, F32)
    outs = ([row_out(w, F32) for w in (SSD_WIDTH, SSD_XBC, 3 * SC_WIDTH)] + [row_out(ATT_WIDTH, BF16)] * 2
            + [kv_out] * 6 + [row_out(LANES, F32)] + [row_out(KV_WIDTH, F32)] * 2 + [row_out(KV_WIDTH, BF16)] * 4)
    return pl.pallas_call(
        functools.partial(_inproj_kernel, tm=tm, seq_len=seq_len, pos0=pos0, dims_major=dims_major),
        grid=(t // tm,),
        in_specs=[row_spec(D_MODEL), const_spec(norm_w), const_spec(w_perm), const_spec(invf)],
        out_specs=[o[0] for o in outs],
        out_shape=[o[1] for o in outs],
        scratch_shapes=[pltpu.VMEM((tm, IN_PAD), F32)],
        compiler_params=pltpu.CompilerParams(dimension_semantics=("arbitrary",), vmem_limit_bytes=VMEM_LIMIT_BYTES),
        name="inproj",
    )(h, norm_w, w_perm, invf)


def _pad_rows(x, rows):
    if x.shape[0] == rows:
        return x
    return jnp.concatenate([x, jnp.zeros((rows - x.shape[0], x.shape[1]), x.dtype)], axis=0)


def _mixer_kernel(z_ref, xbc_ref, sc_ref, small_ref, ssm0_ref, cssd0_ref, csc0_ref,
                  cw_ref, cb_ref, dtb_ref, alog_ref, dskip_ref, nw_ref, scw_ref, tril_ref, expand_ref,
                  yssd_ref, ysc_ref, ssm_out_ref, cssd_out_ref, csc_out_ref,
                  s_scr, xp_scr, scp_scr, *, cs, n_chunks):
    C = SSD_CHUNK
    c = pl.program_id(1)

    @pl.when(c == 0)
    def _():
        s_scr[...] = ssm0_ref[0]
        xp_scr[0:SUBLANES, :] = cssd0_ref[0]
        scp_scr[0:SUBLANES, :] = csc0_ref[0]

    xp_scr[SUBLANES:SUBLANES + C, :] = _pad_rows(xbc_ref[...], C)
    cw = cw_ref[...]
    conv = cb_ref[...]
    for k in range(SSD_CONV):
        off = SUBLANES - (SSD_CONV - 1) + k
        conv = conv + cw[k:k + 1, :] * xp_scr[off:off + C, :]
    xbc_c = _silu(conv)
    tail_ssd = xp_scr[cs:cs + SUBLANES, :]
    xp_scr[0:SUBLANES, :] = tail_ssd

    xs = xbc_c[:, 0:SSD_WIDTH]
    bm = xbc_c[:, SSD_WIDTH:SSD_WIDTH + SSD_GROUPS * SSD_STATE]
    cm = xbc_c[:, SSD_WIDTH + SSD_GROUPS * SSD_STATE:SSD_XBC]

    lane = _iota((C, LANES), 1)
    rowi = _iota((C, LANES), 0)
    small = _pad_rows(small_ref[...], C)
    dt_raw = small + dtb_ref[...]
    dt = jnp.maximum(dt_raw, 0.0) + jnp.log1p(jnp.exp(-jnp.abs(dt_raw)))
    dt = jnp.where((lane < SSD_HEADS) & (rowi < cs), dt, 0.0)
    a_row = jnp.where(lane[0:1, :] < SSD_HEADS, -jnp.exp(alog_ref[...]), 0.0)
    acum = _dot_exact(tril_ref[...], dt * a_row)
    acum_t = acum.T
    a_last = acum[C - 1:C, :]
    expand = expand_ref[...]
    dt_w = _dot_exact(dt, expand)
    eacc_w = _dot_exact(jnp.exp(acum), expand)
    dend_w = _dot_exact(jnp.exp(a_last - acum), expand)
    cdec = jnp.exp(a_last)
    xdt = xs * dt_w
    xdtw_t = (xdt * dend_w).T

    causal = _iota((C, C), 0) >= _iota((C, C), 1)
    head_lo = lane < HEAD_DIM
    zeros_half = jnp.zeros((HEAD_DIM, SSD_STATE), BF16)
    y_chunks = [None] * (SSD_WIDTH // LANES)
    for g in range(SSD_GROUPS):
        bm_g = bm[:, g * SSD_STATE:(g + 1) * SSD_STATE].astype(BF16)
        cm_g = cm[:, g * SSD_STATE:(g + 1) * SSD_STATE].astype(BF16)
        cb = _dot_nt(cm_g, bm_g)
        for r in range(SSD_HEADS // SSD_GROUPS):
            h = g * (SSD_HEADS // SSD_GROUPS) + r
            ch, lo = h // 2, h % 2 == 0
            keep = head_lo if lo else jnp.logical_not(head_lo)
            seg = acum[:, h:h + 1] - acum_t[h:h + 1, :]
            decay = jnp.where(causal, jnp.exp(jnp.where(causal, seg, 0.0)), 0.0)
            xdt_h = jnp.where(keep, xdt[:, ch * LANES:(ch + 1) * LANES], 0.0).astype(BF16)
            y_d = _dot((cb * decay).astype(BF16), xdt_h)
            s_h = s_scr[h]
            s_b = s_h.astype(BF16)
            s_pl = jnp.concatenate([s_b, zeros_half] if lo else [zeros_half, s_b], axis=0)
            y_o = _dot_nt(cm_g, s_pl)
            y_h = y_d + eacc_w[:, ch * LANES:(ch + 1) * LANES] * y_o
            y_chunks[ch] = y_h if y_chunks[ch] is None else y_chunks[ch] + y_h
            s_scr[h] = s_h * cdec[:, h:h + 1] + _dot(xdtw_t[h * HEAD_DIM:(h + 1) * HEAD_DIM, :].astype(BF16), bm_g)
    y = jnp.concatenate(y_chunks, axis=1) + dskip_ref[...] * xs
    gated = y * _silu(_pad_rows(z_ref[...], C))
    yssd_ref[...] = _rms(gated, nw_ref[...])[0:cs]

    sc = _pad_rows(sc_ref[...], C)
    scp_scr[SUBLANES:SUBLANES + C, :] = sc[:, SC_WIDTH:2 * SC_WIDTH] * sc[:, 2 * SC_WIDTH:3 * SC_WIDTH]
    scw = scw_ref[...]
    conv3 = jnp.zeros((C, SC_WIDTH), F32)
    for k in range(SC_CONV):
        off = SUBLANES - (SC_CONV - 1) + k
        conv3 = conv3 + scw[k:k + 1, :] * scp_scr[off:off + C, :]
    ysc_ref[...] = (sc[:, 0:SC_WIDTH] * conv3)[0:cs]
    tail_sc = scp_scr[cs:cs + SUBLANES, :]
    scp_scr[0:SUBLANES, :] = tail_sc

    @pl.when(c == n_chunks - 1)
    def _():
        ssm_out_ref[0] = s_scr[...]
        cssd_out_ref[0] = tail_ssd
        csc_out_ref[0] = tail_sc


def _mixer(z, xbc, sc, small, ssm0, cssd0, csc0, lw, *, batch, seq_len):
    cs = min(SSD_CHUNK, seq_len)
    assert seq_len % cs == 0 and cs % SUBLANES == 0
    nc = seq_len // cs
    t = batch * seq_len
    row_spec = lambda w: pl.BlockSpec((cs, w), lambda b, c: (b * nc + c, 0))
    const_spec = lambda a: pl.BlockSpec(a.shape, lambda b, c: (0,) * a.ndim)
    batch_spec = lambda a: pl.BlockSpec((1,) + a.shape[1:], lambda b, c: (b,) + (0,) * (a.ndim - 1))
    consts = [lw["ssd_conv_w"], lw["ssd_conv_b"], lw["dt_bias"], lw["a_log"], lw["d_skip"], lw["ssd_norm_w"],
              lw["sc_conv_w"], lw["tril"], lw["expand"]]
    out_shape = [jax.ShapeDtypeStruct((t, SSD_WIDTH), F32), jax.ShapeDtypeStruct((t, SC_WIDTH), F32),
                 jax.ShapeDtypeStruct(ssm0.shape, F32), jax.ShapeDtypeStruct(cssd0.shape, F32),
                 jax.ShapeDtypeStruct(csc0.shape, F32)]
    return pl.pallas_call(
        functools.partial(_mixer_kernel, cs=cs, n_chunks=nc),
        grid=(batch, nc),
        in_specs=[row_spec(SSD_WIDTH), row_spec(SSD_XBC), row_spec(3 * SC_WIDTH), row_spec(LANES),
                  batch_spec(ssm0), batch_spec(cssd0), batch_spec(csc0)] + [const_spec(a) for a in consts],
        out_specs=[row_spec(SSD_WIDTH), row_spec(SC_WIDTH), batch_spec(ssm0), batch_spec(cssd0), batch_spec(csc0)],
        out_shape=out_shape,
        scratch_shapes=[pltpu.VMEM((SSD_HEADS, HEAD_DIM, SSD_STATE), F32),
                        pltpu.VMEM((SUBLANES + SSD_CHUNK, SSD_XBC), F32),
                        pltpu.VMEM((SUBLANES + SSD_CHUNK, SC_WIDTH), F32)],
        compiler_params=pltpu.CompilerParams(dimension_semantics=("arbitrary", "arbitrary"),
                                             vmem_limit_bytes=VMEM_LIMIT_BYTES),
        name="mixer",
    )(z, xbc, sc, small, ssm0, cssd0, csc0, *consts)


def _compress_units(units, pe0, pe1, w_first, w_second):
    n_units = units.shape[0]
    first = _dot((units + pe0).astype(BF16), w_first)
    second = _dot((units + pe1).astype(BF16), w_second)
    return _silu(first + pltpu.roll(second, n_units - 1, 0))


def _compress_prompt_kernel(kc_ref, vc_ref, pe_ref, wf_ref, ws_ref, w2_ref, w2t_ref, ck_ref, cvt_ref):
    hid_k = _compress_units(kc_ref[0], pe_ref[0, 0:1, :], pe_ref[0, 1:2, :], wf_ref[0], ws_ref[0])
    ck_ref[0] = _dot(hid_k.astype(BF16), w2_ref[0]).astype(BF16)
    hid_v = _compress_units(vc_ref[0], pe_ref[1, 0:1, :], pe_ref[1, 1:2, :], wf_ref[1], ws_ref[1])
    cvt_ref[0] = _dot(w2t_ref[1], hid_v.T.astype(BF16)).astype(BF16)


def _compress_prompt(kc, vc, lw, *, batch, seq_len):
    n_units = seq_len // CMP_STRIDE
    unit_w = CMP_STRIDE * KV_WIDTH
    kc = kc.reshape(batch, n_units, unit_w)
    vc = vc.reshape(batch, n_units, unit_w)
    consts = [lw["cmp_pe"], lw["cmp_wf"], lw["cmp_ws"], lw["cmp_w2"], lw["cmp_w2t"]]
    const_spec = lambda a: pl.BlockSpec(a.shape, lambda b: (0,) * a.ndim)
    in_spec = pl.BlockSpec((1, n_units, unit_w), lambda b: (b, 0, 0))
    return pl.pallas_call(
        _compress_prompt_kernel,
        grid=(batch,),
        in_specs=[in_spec, in_spec] + [const_spec(a) for a in consts],
        out_specs=[pl.BlockSpec((1, n_units, KV_WIDTH), lambda b: (b, 0, 0)),
                   pl.BlockSpec((1, KV_WIDTH, n_units), lambda b: (b, 0, 0))],
        out_shape=[jax.ShapeDtypeStruct((batch, n_units, KV_WIDTH), BF16),
                   jax.ShapeDtypeStruct((batch, KV_WIDTH, n_units), BF16)],
        compiler_params=pltpu.CompilerParams(dimension_semantics=("arbitrary",), vmem_limit_bytes=VMEM_LIMIT_BYTES),
        name="compress_prompt",
    )(kc, vc, *consts)


def _page_copy(cache_hbm, layer, page, buf, slot, p, sem):
    return pltpu.make_async_copy(cache_hbm.at[layer, page], buf.at[slot, :, pl.ds(p * PAGE_SIZE, PAGE_SIZE)], sem)


def _gather_pages(pt_ref, b, cache_hbm, layer, buf, slot, sem, n_pages):
    def start(p, carry):
        _page_copy(cache_hbm, layer, pt_ref[b, p], buf, slot, p, sem).start()
        return carry
    lax.fori_loop(0, n_pages, start, 0)


def _wait_pages(cache_hbm, layer, buf, slot, sem, n_pages):
    def wait(p, carry):
        _page_copy(cache_hbm, layer, 0, buf, slot, p, sem).wait()
        return carry
    lax.fori_loop(0, n_pages, wait, 0)


def _prefetch_pages(pt_ref, cache_hbm, layer, buf, sems, n_pages):
    b = pl.program_id(0)
    slot = b % 2

    @pl.when(b == 0)
    def _():
        _gather_pages(pt_ref, 0, cache_hbm, layer, buf, 0, sems.at[0], n_pages)

    @pl.when(b + 1 < pl.num_programs(0))
    def _():
        _gather_pages(pt_ref, b + 1, cache_hbm, layer, buf, 1 - slot, sems.at[1 - slot], n_pages)
    return slot


def _compress_paged_kernel(pt_ref, kc_hbm, vc_hbm, pe_ref, wr_ref, w2_ref, ck_ref, cv_ref,
                           kbuf, vbuf, rows_scr, ksems, vsems, *, layer, n_pages):
    n_units = n_pages * (PAGE_SIZE // CMP_STRIDE)
    slot = _prefetch_pages(pt_ref, kc_hbm, layer, kbuf, ksems, n_pages)
    _prefetch_pages(pt_ref, vc_hbm, layer, vbuf, vsems, n_pages)
    for idx, (src_hbm, buf, sems, dst) in enumerate(((kc_hbm, kbuf, ksems, ck_ref), (vc_hbm, vbuf, vsems, cv_ref))):
        _wait_pages(src_hbm, layer, buf, slot, sems.at[slot], n_pages)

        def to_rows(c, carry, buf=buf):
            for p in range(pages_per_trip):
                off = pl.multiple_of((c * pages_per_trip + p) * PAGE_SIZE, PAGE_SIZE)
                rows_scr[pl.ds(off, PAGE_SIZE), :] = buf[slot, :, pl.ds(off, PAGE_SIZE)].T
            return carry
        pages_per_trip = math.gcd(n_pages, 8)
        lax.fori_loop(0, n_pages // pages_per_trip, to_rows, 0)

        units = jnp.concatenate([rows_scr[pl.ds(r, n_units, stride=CMP_STRIDE), :].astype(BF16)
                                 for r in range(CMP_STRIDE)], axis=1)
        acc = _dot(units, wr_ref[idx])
        pe_rows = jnp.concatenate([jnp.broadcast_to(pe_ref[idx, h:h + 1, :], (SUBLANES // 2, pe_ref.shape[2]))
                                   for h in range(2)], axis=0)
        pe_both = _dot(pe_rows.astype(BF16), wr_ref[idx])
        pe_term = pe_both[0:1, 0:KV_WIDTH] + pe_both[SUBLANES // 2:SUBLANES // 2 + 1, KV_WIDTH:2 * KV_WIDTH]
        pre = acc[:, 0:KV_WIDTH] + pltpu.roll(acc[:, KV_WIDTH:2 * KV_WIDTH], n_units - 1, 0) + pe_term
        dst[0] = _dot(_silu(pre).astype(BF16), w2_ref[idx]).astype(BF16)


def _compress_paged(page_table, cache_kt, cache_vt, lw, *, layer):
    batch, n_pages = page_table.shape
    past = n_pages * PAGE_SIZE
    n_units = past // CMP_STRIDE
    consts = [lw["cmp_pe"], lw["cmp_wr"], lw["cmp_w2"]]
    const_spec = lambda a: pl.BlockSpec(a.shape, lambda b, pt: (0,) * a.ndim)
    any_spec = pl.BlockSpec(memory_space=pl.ANY)
    out_spec = pl.BlockSpec((1, n_units, KV_WIDTH), lambda b, pt: (b, 0, 0))
    grid_spec = pltpu.PrefetchScalarGridSpec(
        num_scalar_prefetch=1, grid=(batch,),
        in_specs=[any_spec, any_spec] + [const_spec(a) for a in consts],
        out_specs=[out_spec, out_spec],
        scratch_shapes=[pltpu.VMEM((2, KV_WIDTH, past), F32), pltpu.VMEM((2, KV_WIDTH, past), F32),
                        pltpu.VMEM((past, KV_WIDTH), F32),
                        pltpu.SemaphoreType.DMA((2,)), pltpu.SemaphoreType.DMA((2,))])
    return pl.pallas_call(
        functools.partial(_compress_paged_kernel, layer=layer, n_pages=n_pages),
        grid_spec=grid_spec,
        out_shape=[jax.ShapeDtypeStruct((batch, n_units, KV_WIDTH), BF16)] * 2,
        compiler_params=pltpu.CompilerParams(dimension_semantics=("arbitrary",), vmem_limit_bytes=VMEM_LIMIT_BYTES),
        name="compress_paged",
    )(page_table, cache_kt, cache_vt, *consts)


def _stack_heads(q):
    lane = _iota((q.shape[0], LANES), 1)
    lo = lane < HEAD_DIM
    c0, c1, c2 = (q[:, c * LANES:(c + 1) * LANES] for c in range(3))
    blocks = [jnp.where(lo, c0, 0.0), jnp.where(lo, pltpu.roll(c0, HEAD_DIM, 1), 0.0), jnp.where(lo, c1, 0.0),
              jnp.where(lo, 0.0, c1), jnp.where(lo, 0.0, pltpu.roll(c2, HEAD_DIM, 1)), jnp.where(lo, 0.0, c2)]
    return jnp.concatenate(blocks, axis=0)


def _unstack_heads(o, rows):
    lane = _iota((rows, LANES), 1)
    lo = lane < HEAD_DIM
    blk = [o[j * rows:(j + 1) * rows] for j in range(ATT_HEADS)]
    return jnp.concatenate([jnp.where(lo, blk[0], pltpu.roll(blk[1], HEAD_DIM, 1)),
                            jnp.where(lo, blk[2], blk[3]),
                            jnp.where(lo, pltpu.roll(blk[4], HEAD_DIM, 1), blk[5])], axis=1)


def _tile_heads(x, reps):
    return jnp.concatenate([x] * reps, axis=0)


def _masked_softmax(s, mask):
    s = jnp.where(mask, s, NEG)
    e = jnp.where(mask, jnp.exp(s - jnp.max(s, axis=1, keepdims=True)), 0.0)
    l = jnp.sum(e, axis=1, keepdims=True)
    return e * (1.0 / jnp.where(l > 0.0, l, 1.0))


def _split_bf16(x):
    hi = x.astype(BF16)
    return hi, (x - hi.astype(F32)).astype(BF16)


def _select_blocks(imp, tq, axis):
    n_blocks = imp.shape[axis]
    blk = _iota(imp.shape, axis)
    cur = tq >> 6
    causal = blk * SEL_BLOCK <= tq
    forced = causal & ((blk == 0) | (blk == cur) | (blk == cur - 1))
    score = jnp.where(forced, FORCE, jnp.where(causal, imp, -FORCE))
    blk_f = blk.astype(F32)
    sel = jnp.zeros(imp.shape, F32)
    for _ in range(N_SEL):
        mx = jnp.max(score, axis=axis, keepdims=True)
        first = jnp.min(jnp.where(score == mx, blk_f, float(n_blocks)), axis=axis, keepdims=True)
        pick = blk_f == first
        sel = jnp.where(pick & (mx > -FORCE / 2), 1.0, sel)
        score = jnp.where(pick, -jnp.inf, score)
    return sel


def _combine_gates(gates, o_c, o_s, o_w, rows):
    out = []
    for hh in range(ATT_HEADS):
        sl = slice(hh * rows, (hh + 1) * rows)
        g0 = GATE_LANE0 + 3 * hh
        out.append(gates[:, g0:g0 + 1] * o_c[sl] + gates[:, g0 + 1:g0 + 2] * o_s[sl] + gates[:, g0 + 2:g0 + 3] * o_w[sl])
    return jnp.concatenate(out, axis=0)


def _attn_prompt_kernel(qr_ref, qn_ref, small_ref, ck_ref, cvt_ref, ks_ref, vs_ref, kw_ref, vw_ref,
                        onehot_ref, mimp_ref, out_ref,
                        qaug_scr, sa_scr, sb_scr, p_scr, m_scr, l_scr, a_scr, acc_scr,
                        sct_scr, pct_scr, psum_scr, sw_scr, pw_scr, lw_scr, oc_scr, ow_scr):
    R = Q_BLOCK
    TK = KEYS_PER_TILE
    i = pl.program_id(1)
    t0 = pl.multiple_of(i * R, R)
    scale = HEAD_DIM ** -0.5
    tq = t0 + _iota((R, 1), 0)
    tq_row = t0 + _iota((1, R), 1)
    q_rope = _stack_heads(qr_ref[...].astype(F32) * scale).astype(BF16)
    q_raw = _stack_heads(qn_ref[...].astype(F32) * scale).astype(BF16)
    qaug_scr[:, 0:LANES] = q_rope

    span = WINDOW + R
    s0 = pl.multiple_of(jnp.maximum(t0 - WINDOW, 0), R)
    sw_scr[...] = _dot_nt(q_rope, kw_ref[pl.ds(s0, span), :])
    spos = s0 + _iota((1, span), 1)
    win_bias = jnp.where((spos <= tq) & (spos > tq - WINDOW), 0.0, NEG)
    for j in range(ATT_HEADS):
        rows = slice(j * R, (j + 1) * R)
        s = sw_scr[rows, :] + win_bias
        e = jnp.exp(s - jnp.max(s, axis=1, keepdims=True))
        pw_scr[rows, :] = e.astype(BF16)
        lw_scr[rows, :] = jnp.broadcast_to(jnp.sum(e, axis=1, keepdims=True), (R, LANES))
    ow_scr[...] = _dot(pw_scr[...], vw_ref[pl.ds(s0, span), :]) * (1.0 / lw_scr[...])

    n_units = ck_ref.shape[1]
    sct_scr[...] = _dot_nt(ck_ref[0], q_raw)
    cmp_end = _iota((n_units, 1), 0) * CMP_STRIDE + (CMP_LEN - 1)
    cmp_bias = jnp.where(cmp_end <= tq_row, 0.0, NEG)
    for j in range(ATT_HEADS):
        cols = slice(j * R, (j + 1) * R)
        s = sct_scr[:, cols] + cmp_bias
        mx = jnp.max(s, axis=0, keepdims=True)
        e = jnp.exp(s - mx)
        inv = jnp.where(mx > NEG / 2, 1.0 / jnp.sum(e, axis=0, keepdims=True), 0.0)
        p = e * inv
        pct_scr[:, cols] = p.astype(BF16)
        k, g = divmod(j, GQA)
        if g == 0:
            psum_scr[k] = p
        else:
            psum_scr[k] += p
    o_c_t = _dot(cvt_ref[0], pct_scr[...])
    oc_scr[...] = jnp.concatenate([o_c_t[:, j * R:(j + 1) * R].T for j in range(ATT_HEADS)], axis=0)

    for k in range(ATT_KV_HEADS):
        hi, lo = _split_bf16(psum_scr[k])
        imp_t = _dot(mimp_ref[...], hi) + _dot(mimp_ref[...], lo)
        sel_t = _select_blocks(imp_t, tq_row, 0)
        bias = jnp.where(sel_t.T > 0.5, 0.0, NEG).astype(BF16)
        for g in range(GQA):
            j = k * GQA + g
            qaug_scr[j * R:(j + 1) * R, LANES:2 * LANES] = bias

    def scores(k0, s_ref):
        k_aug = jnp.concatenate([ks_ref[pl.ds(k0, TK), :], onehot_ref[pl.ds(k0, TK), :]], axis=1)
        s_ref[...] = _dot_nt(qaug_scr[...], k_aug)

    def accumulate(k0, s_ref, first):
        if first:
            causal_bias = jnp.where((k0 + _iota((1, TK), 1)) <= tq, 0.0, NEG)
        for j in range(ATT_HEADS):
            rows = slice(j * R, (j + 1) * R)
            s = s_ref[rows, :]
            if first:
                s = s + causal_bias
                m_new = jnp.broadcast_to(jnp.max(s, axis=1, keepdims=True), (R, LANES))
            else:
                m_old = m_scr[rows, :]
                m_new = jnp.maximum(m_old, jnp.max(s, axis=1, keepdims=True))
                alpha = jnp.exp(m_old - m_new)
            p = jnp.exp(s - jnp.concatenate([m_new] * (TK // LANES), axis=1))
            l_new = jnp.sum(p, axis=1, keepdims=True)
            p_scr[rows, :] = p.astype(BF16)
            m_scr[rows, :] = m_new
            if first:
                l_scr[rows, :] = jnp.broadcast_to(l_new, (R, LANES))
            else:
                l_scr[rows, :] = alpha * l_scr[rows, :] + l_new
                a_scr[rows, :] = alpha
        pv = _dot(p_scr[...], vs_ref[pl.ds(k0, TK), :])
        if first:
            acc_scr[...] = pv
        else:
            acc_scr[...] = a_scr[...] * acc_scr[...] + pv

    n_past = t0 // TK
    k_diag = pl.multiple_of(n_past * TK, TK)
    scores(k_diag, sa_scr)
    accumulate(k_diag, sa_scr, True)

    @pl.when(n_past % 2 == 1)
    def _():
        k_odd = pl.multiple_of((n_past - 1) * TK, TK)
        scores(k_odd, sb_scr)
        accumulate(k_odd, sb_scr, False)

    scores(0, sa_scr)

    def pair_body(jj, carry):
        k_even = pl.multiple_of(2 * jj * TK, TK)
        k_odd = pl.multiple_of(k_even + TK, TK)
        scores(k_odd, sb_scr)
        accumulate(k_even, sa_scr, False)
        scores(pl.multiple_of(k_odd + TK, TK), sa_scr)
        accumulate(k_odd, sb_scr, False)
        return carry
    lax.fori_loop(0, n_past // 2, pair_body, 0)
    o_s = acc_scr[...] * (1.0 / l_scr[...])

    out_ref[...] = _unstack_heads(_combine_gates(small_ref[...], oc_scr[...], o_s, ow_scr[...], R), R)


def _attn_prompt(qrope, qraw, small, ck, cvt, kselb, vselb, kwinb, vwinb, *, batch, seq_len):
    span = WINDOW + Q_BLOCK
    assert seq_len % KEYS_PER_TILE == 0 and seq_len >= span and seq_len // SEL_BLOCK <= LANES
    nqb = seq_len // Q_BLOCK
    n_units = seq_len // CMP_STRIDE
    t = batch * seq_len
    pos = jnp.arange(seq_len, dtype=jnp.int32)
    onehot = (pos[:, None] // SEL_BLOCK == jnp.arange(LANES, dtype=jnp.int32)[None, :]).astype(BF16)
    m_imp_t = _importance_matrix(n_units, LANES).T
    q_spec = lambda w: pl.BlockSpec((Q_BLOCK, w), lambda b, i: (b * nqb + i, 0))
    batch_spec = lambda a: pl.BlockSpec((1,) + a.shape[1:], lambda b, i: (b, 0, 0))
    seq_spec = pl.BlockSpec((seq_len, KV_WIDTH), lambda b, i: (b, 0))
    const_spec = lambda a: pl.BlockSpec(a.shape, lambda b, i: (0, 0))
    rows = ATT_HEADS * Q_BLOCK
    vmem = pltpu.VMEM
    scratch = [vmem((rows, 2 * LANES), BF16),
               vmem((rows, KEYS_PER_TILE), F32), vmem((rows, KEYS_PER_TILE), F32),
               vmem((rows, KEYS_PER_TILE), BF16),
               vmem((rows, LANES), F32), vmem((rows, LANES), F32), vmem((rows, LANES), F32),
               vmem((rows, LANES), F32),
               vmem((n_units, rows), F32), vmem((n_units, rows), BF16),
               vmem((ATT_KV_HEADS, n_units, Q_BLOCK), F32),
               vmem((rows, span), F32), vmem((rows, span), BF16), vmem((rows, LANES), F32),
               vmem((rows, LANES), F32), vmem((rows, LANES), F32)]
    return pl.pallas_call(
        _attn_prompt_kernel,
        grid=(batch, nqb),
        in_specs=[q_spec(ATT_WIDTH), q_spec(ATT_WIDTH), q_spec(LANES), batch_spec(ck), batch_spec(cvt),
                  seq_spec, seq_spec, seq_spec, seq_spec, const_spec(onehot), const_spec(m_imp_t)],
        out_specs=q_spec(ATT_WIDTH),
        out_shape=jax.ShapeDtypeStruct((t, ATT_WIDTH), F32),
        scratch_shapes=scratch,
        compiler_params=pltpu.CompilerParams(dimension_semantics=("arbitrary", "arbitrary"),
                                             vmem_limit_bytes=VMEM_LIMIT_BYTES),
        name="attn_prompt",
    )(qrope, qraw, small, ck, cvt, kselb, vselb, kwinb, vwinb, onehot, m_imp_t)


def _importance_matrix(n_units, width):
    n = jnp.arange(n_units, dtype=jnp.int32)[:, None]
    j = jnp.arange(width, dtype=jnp.int32)[None, :]
    upb = SEL_BLOCK // CMP_STRIDE
    m = (n // upb == j).astype(F32) + ((n + 1) // upb == j).astype(F32)
    return jnp.where(n < n_units - 1, m, 0.0).astype(BF16)


def _attn_sample_kernel(pt_ref, ks_hbm, vs_hbm, qr_ref, qn_ref, small_ref, ck_ref, cv_ref,
                        knew_ref, vnew_ref, kwc_ref, vwc_ref, kwnew_ref, vwnew_ref, onehot_ref, mimp_ref,
                        out_ref, kbuf, vbuf, ksems, vsems, *, layer, n_pages, past, win_buf, group_blocks):
    R = qr_ref.shape[0]
    slot = _prefetch_pages(pt_ref, ks_hbm, layer, kbuf, ksems, n_pages)
    _prefetch_pages(pt_ref, vs_hbm, layer, vbuf, vsems, n_pages)

    scale = HEAD_DIM ** -0.5
    tq = past + _iota((R, 1), 0)
    tq_all = _tile_heads(tq, ATT_HEADS)
    q_rope = _stack_heads(qr_ref[...].astype(F32) * scale).astype(BF16)
    q_raw = _stack_heads(qn_ref[...].astype(F32) * scale).astype(BF16)

    n_units = ck_ref.shape[1]
    s_c = _dot_nt(q_raw, ck_ref[0])
    cmp_end = _iota((1, n_units), 1) * CMP_STRIDE + (CMP_LEN - 1)
    p_c = _masked_softmax(s_c, cmp_end <= tq_all)
    o_c = _dot(p_c.astype(BF16), cv_ref[0])

    bias = []
    for k in range(ATT_KV_HEADS):
        hi, lo = _split_bf16(sum(p_c[(k * GQA + g) * R:(k * GQA + g + 1) * R] for g in range(GQA)))
        sel = _select_blocks(_dot(hi, mimp_ref[...]) + _dot(lo, mimp_ref[...]), tq, 1)
        bias.append(_tile_heads(jnp.where(sel > 0.5, 0.0, NEG), GQA))
    bias = jnp.concatenate(bias, axis=0).astype(BF16)

    k_off = past - win_buf
    s_wc = _dot(q_rope, kwc_ref[0, 0].astype(BF16))
    s_wn = _dot_nt(q_rope, kwnew_ref[...])
    pos_c = k_off + _iota((1, win_buf), 1)
    pos_n = past + _iota((1, R), 1)
    mask_c = (pos_c <= tq_all) & (pos_c > tq_all - WINDOW) & (pos_c >= k_off)
    mask_n = (pos_n <= tq_all) & (pos_n > tq_all - WINDOW) & (pos_n >= k_off)
    s_wc = jnp.where(mask_c, s_wc, NEG)
    s_wn = jnp.where(mask_n, s_wn, NEG)
    m_w = jnp.maximum(jnp.max(s_wc, axis=1, keepdims=True), jnp.max(s_wn, axis=1, keepdims=True))
    e_wc = jnp.where(mask_c, jnp.exp(s_wc - m_w), 0.0)
    e_wn = jnp.where(mask_n, jnp.exp(s_wn - m_w), 0.0)
    l_w = jnp.sum(e_wc, axis=1, keepdims=True) + jnp.sum(e_wn, axis=1, keepdims=True)
    o_w = _dot_nt(e_wc.astype(BF16), vwc_ref[0, 0].astype(BF16)) + _dot(e_wn.astype(BF16), vwnew_ref[...])
    o_w = o_w * (1.0 / jnp.where(l_w > 0.0, l_w, 1.0))

    s_n = _dot_nt(q_rope, knew_ref[...])
    mask_sn = pos_n <= tq_all
    s_n = jnp.where(mask_sn, s_n, NEG)
    _wait_pages(ks_hbm, layer, kbuf, slot, ksems.at[slot], n_pages)
    group_keys = group_blocks * SEL_BLOCK
    n_groups = (n_pages * PAGE_SIZE) // group_keys
    s_past = []
    for gi in range(n_groups):
        keys = slice(gi * group_keys, (gi + 1) * group_keys)
        q_aug = jnp.concatenate([q_rope, bias[:, gi * group_blocks:(gi + 1) * group_blocks]], axis=1)
        k_aug_t = jnp.concatenate([kbuf[slot, :, keys].astype(BF16), onehot_ref[...]], axis=0)
        s_past.append(_dot(q_aug, k_aug_t))
    m_s = jnp.max(s_n, axis=1, keepdims=True)
    for s in s_past:
        m_s = jnp.maximum(m_s, jnp.max(s, axis=1, keepdims=True))
    e_n = jnp.where(mask_sn, jnp.exp(s_n - m_s), 0.0)
    l_s = jnp.sum(e_n, axis=1, keepdims=True)
    o_s = _dot(e_n.astype(BF16), vnew_ref[...])
    _wait_pages(vs_hbm, layer, vbuf, slot, vsems.at[slot], n_pages)
    for gi, s in enumerate(s_past):
        e = jnp.exp(s - m_s)
        l_s = l_s + jnp.sum(e, axis=1, keepdims=True)
        o_s = o_s + _dot_nt(e.astype(BF16), vbuf[slot, :, gi * group_keys:(gi + 1) * group_keys].astype(BF16))
    o_s = o_s * (1.0 / l_s)

    out_ref[...] = _unstack_heads(_combine_gates(small_ref[...], o_c, o_s, o_w, R), R)


def _attn_sample(page_table, cache_kst, cache_vst, cache_kwt, cache_vwt, qrope, qraw, small, ck, cv,
                 knew, vnew, kwnew, vwnew, *, layer, seq_len):
    batch, n_pages = page_table.shape
    past = n_pages * PAGE_SIZE
    win_buf = cache_kwt.shape[3]
    assert seq_len == SUBLANES and (past + seq_len) // CMP_STRIDE == past // CMP_STRIDE and past >= WINDOW
    n_units = past // CMP_STRIDE
    nb_past = past // SEL_BLOCK
    group_blocks = min(LANES, nb_past)
    assert nb_past % group_blocks == 0
    n_blk_lanes = -(-(nb_past + 1) // LANES) * LANES
    key = jnp.arange(group_blocks * SEL_BLOCK, dtype=jnp.int32)
    onehot = (key[None, :] // SEL_BLOCK == jnp.arange(group_blocks, dtype=jnp.int32)[:, None]).astype(BF16)
    m_imp = _importance_matrix(n_units, n_blk_lanes)
    win_spec = pl.BlockSpec((1, 1, KV_WIDTH, win_buf), lambda b, pt: (layer, b, 0, 0))
    any_spec = pl.BlockSpec(memory_space=pl.ANY)
    row_spec = lambda w: pl.BlockSpec((seq_len, w), lambda b, pt: (b, 0))
    batch_spec = lambda a: pl.BlockSpec((1,) + a.shape[1:], lambda b, pt: (b, 0, 0))
    const_spec = lambda a: pl.BlockSpec(a.shape, lambda b, pt: (0, 0))
    grid_spec = pltpu.PrefetchScalarGridSpec(
        num_scalar_prefetch=1, grid=(batch,),
        in_specs=[any_spec, any_spec, row_spec(ATT_WIDTH), row_spec(ATT_WIDTH), row_spec(LANES),
                  batch_spec(ck), batch_spec(cv), row_spec(KV_WIDTH), row_spec(KV_WIDTH),
                  win_spec, win_spec, row_spec(KV_WIDTH), row_spec(KV_WIDTH),
                  const_spec(onehot), const_spec(m_imp)],
        out_specs=row_spec(ATT_WIDTH),
        scratch_shapes=[pltpu.VMEM((2, KV_WIDTH, past), F32), pltpu.VMEM((2, KV_WIDTH, past), F32),
                        pltpu.SemaphoreType.DMA((2,)), pltpu.SemaphoreType.DMA((2,))])
    return pl.pallas_call(
        functools.partial(_attn_sample_kernel, layer=layer, n_pages=n_pages, past=past, win_buf=win_buf,
                          group_blocks=group_blocks),
        grid_spec=grid_spec,
        out_shape=jax.ShapeDtypeStruct((batch * seq_len, ATT_WIDTH), F32),
        compiler_params=pltpu.CompilerParams(dimension_semantics=("arbitrary",), vmem_limit_bytes=VMEM_LIMIT_BYTES),
        name="attn_sample",
    )(page_table, cache_kst, cache_vst, qrope, qraw, small, ck, cv, knew, vnew, cache_kwt, cache_vwt, kwnew, vwnew,
      onehot, m_imp)


def _ffn_kernel(h_ref, yssd_ref, ysc_ref, yatt_ref, n2_ref, fn_ref, wo_hbm, w1_hbm, w2_hbm, out_ref,
                wo_scr, w1_scr, w2_scr, hid_scr, sems, *, final):
    @pl.when(pl.program_id(0) == 0)
    def _():
        copies = [pltpu.make_async_copy(src, dst, sems.at[n])
                  for n, (src, dst) in enumerate(((wo_hbm, wo_scr), (w1_hbm, w1_scr), (w2_hbm, w2_scr)))]
        for c in copies:
            c.start()
        for c in copies:
            c.wait()

    mix = (_dot(yssd_ref[...].astype(BF16), wo_scr[0:SSD_WIDTH, :])
           + _dot(ysc_ref[...].astype(BF16), wo_scr[SSD_WIDTH:SSD_WIDTH + SC_WIDTH, :])
           + _dot(yatt_ref[...].astype(BF16), wo_scr[SSD_WIDTH + SC_WIDTH:D_MODEL, :]))
    hn = h_ref[...] + mix
    xn = _rms(hn, n2_ref[...]).astype(BF16)
    hid_scr[...] = jnp.square(jnp.maximum(_dot(xn, w1_scr[...]), 0.0)).astype(BF16)
    o = hn + _dot(hid_scr[...], w2_scr[...])
    out_ref[...] = _rms(o, fn_ref[...]) if final else o


def _ffn(h, yssd, ysc, yatt, lw, final_norm_w, *, final):
    t = h.shape[0]
    tm = min(FFN_ROWS, t)
    assert t % tm == 0
    row_spec = lambda w: pl.BlockSpec((tm, w), lambda i: (i, 0))
    const_spec = lambda a: pl.BlockSpec(a.shape, lambda i: (0, 0))
    any_spec = pl.BlockSpec(memory_space=pl.ANY)
    return pl.pallas_call(
        functools.partial(_ffn_kernel, final=final),
        grid=(t // tm,),
        in_specs=[row_spec(D_MODEL), row_spec(SSD_WIDTH), row_spec(SC_WIDTH), row_spec(ATT_WIDTH),
                  const_spec(lw["norm2_w"]), const_spec(final_norm_w), any_spec, any_spec, any_spec],
        out_specs=row_spec(D_MODEL),
        out_shape=jax.ShapeDtypeStruct((t, D_MODEL), F32),
        scratch_shapes=[pltpu.VMEM((D_MODEL, D_MODEL), BF16), pltpu.VMEM((D_MODEL, D_FF), BF16),
                        pltpu.VMEM((D_FF, D_MODEL), BF16), pltpu.VMEM((tm, D_FF), BF16),
                        pltpu.SemaphoreType.DMA((3,))],
        compiler_params=pltpu.CompilerParams(dimension_semantics=("arbitrary",), vmem_limit_bytes=VMEM_LIMIT_BYTES),
        name="ffn",
    )(h, yssd, ysc, yatt, lw["norm2_w"], final_norm_w, lw["w_out"], lw["w_ff1"], lw["w_ff2"])


def _lane_row(v, width):
    v = v.reshape(1, -1).astype(F32)
    return jnp.pad(v, ((0, 0), (0, width - v.shape[1])))


def _prep_layer(l, norm1_w, w_in, ssd_conv_w, ssd_conv_b, ssd_dt_bias, ssd_a_log, ssd_d, ssd_norm_w, sc_conv_w,
                cmp_pe, cmp_w1, cmp_w2, w_out, norm2_w, w_ff1, w_ff2):
    offs = [0]
    for s in IN_SIZES:
        offs.append(offs[-1] + s)
    seg = lambda a: jnp.arange(offs[a], offs[a + 1])
    perm = jnp.concatenate([seg(0), seg(1), seg(3), seg(4), seg(5), seg(6)] + [seg(a) for a in range(7, 13)]
                           + [seg(2), seg(13)])
    w_perm = jnp.pad(w_in[l][:, perm], ((0, 0), (0, IN_PAD - offs[-1]))).astype(BF16)

    eye = jnp.eye(ATT_KV_HEADS, dtype=F32)
    w1 = cmp_w1[l].reshape(2, 2, CMP_STRIDE, HEAD_DIM, HEAD_DIM)
    wfs = jnp.einsum('vhrde,kK->vhrkdKe', w1, eye).reshape(2, 2, CMP_STRIDE * KV_WIDTH, KV_WIDTH).astype(BF16)
    pe = cmp_pe[l].reshape(2, 2, CMP_STRIDE, 1, HEAD_DIM)
    pe = jnp.broadcast_to(pe, (2, 2, CMP_STRIDE, ATT_KV_HEADS, HEAD_DIM)).reshape(2, 2, CMP_STRIDE * KV_WIDTH)
    w2 = jnp.einsum('vde,kK->vkdKe', cmp_w2[l], eye).reshape(2, KV_WIDTH, KV_WIDTH).astype(BF16)

    head = jnp.arange(LANES, dtype=jnp.int32)[:, None]
    col = jnp.arange(SSD_WIDTH, dtype=jnp.int32)[None, :]
    idx = jnp.arange(SSD_CHUNK, dtype=jnp.int32)
    return {
        "norm1_w": norm1_w[l].reshape(1, D_MODEL), "w_in": w_perm,
        "ssd_conv_w": ssd_conv_w[l], "ssd_conv_b": ssd_conv_b[l].reshape(1, SSD_XBC),
        "dt_bias": _lane_row(ssd_dt_bias[l], LANES), "a_log": _lane_row(ssd_a_log[l], LANES),
        "d_skip": jnp.repeat(ssd_d[l], HEAD_DIM).reshape(1, SSD_WIDTH),
        "ssd_norm_w": ssd_norm_w[l].reshape(1, SSD_WIDTH), "sc_conv_w": sc_conv_w[l],
        "tril": (idx[:, None] >= idx[None, :]).astype(F32),
        "expand": (col // HEAD_DIM == head).astype(F32),
        "cmp_pe": pe, "cmp_wf": wfs[:, 0], "cmp_ws": wfs[:, 1], "cmp_w2": w2,
        "cmp_wr": jnp.concatenate([wfs[:, 0], wfs[:, 1]], axis=-1), "cmp_w2t": w2.transpose(0, 2, 1),
        "w_out": w_out[l].astype(BF16), "norm2_w": norm2_w[l].reshape(1, D_MODEL),
        "w_ff1": w_ff1[l].astype(BF16), "w_ff2": w_ff2[l].astype(BF16),
    }


def _front_pad_rows(a, rows):
    return jnp.pad(a, ((0, 0), (rows - a.shape[1], 0), (0, 0)))


def _run_layer(h, lw, invf, ssm0, cssd0, csc0, attn_fn, final_norm_w, *, batch, seq_len, pos0, final):
    (z, xbc, sc, qraw, qrope, kcmp, vcmp, ksel, vsel, kwin, vwin, small, kcmp_rows, vcmp_rows,
     kselb, vselb, kwinb, vwinb) = _inproj(h, lw["norm1_w"], lw["w_in"], invf, batch=batch, seq_len=seq_len, pos0=pos0)
    yssd, ysc, ssm_new, cssd_new, csc_new = _mixer(
        z, xbc, sc, small, ssm0, _front_pad_rows(cssd0, SUBLANES), _front_pad_rows(csc0, SUBLANES), lw,
        batch=batch, seq_len=seq_len)
    rows = dict(kcmp=kcmp, vcmp=vcmp, ksel=ksel, vsel=vsel, kwin=kwin, vwin=vwin, kcmp_rows=kcmp_rows,
                vcmp_rows=vcmp_rows, kselb=kselb, vselb=vselb, kwinb=kwinb, vwinb=vwinb)
    yatt = attn_fn(qrope, qraw, small, rows)
    h = _ffn(h, yssd, ysc, yatt, lw, final_norm_w, final=final)
    states = (ssm_new, cssd_new[:, SUBLANES - (SSD_CONV - 1):], csc_new[:, SUBLANES - (SC_CONV - 1):])
    return h, states, rows


def kernel(x_prompt, x_sample, cache_k_cmp, cache_v_cmp, cache_k_sel, cache_v_sel, cache_k_win, cache_v_win, state_ssm, state_ssd_conv, state_sc_conv, page_table, norm1_w, w_in, ssd_conv_w, ssd_conv_b, ssd_dt_bias, ssd_a_log, ssd_d, ssd_norm_w, sc_conv_w, cmp_pe, cmp_w1, cmp_w2, w_out, norm2_w, w_ff1, w_ff2, final_norm_w):
    bp, lp, _ = x_prompt.shape
    db, ls, _ = x_sample.shape
    depth = w_in.shape[0]
    n_pages = page_table.shape[1]
    past = n_pages * PAGE_SIZE
    win_buf = cache_k_win.shape[2]
    fnw = final_norm_w.reshape(1, D_MODEL)

    half = ROT_DIM // 2
    inv_freq = ROPE_THETA ** (-jnp.arange(half, dtype=F32) * 2.0 / ROT_DIM)
    invf = jnp.tile(inv_freq, LANES // half).reshape(1, LANES)

    hp = x_prompt.reshape(bp * lp, D_MODEL)
    hs = x_sample.reshape(db * ls, D_MODEL)
    ssm0_p = jnp.zeros((bp, SSD_HEADS, HEAD_DIM, SSD_STATE), F32)
    cssd0_p = jnp.zeros((bp, SSD_CONV - 1, SSD_XBC), F32)
    csc0_p = jnp.zeros((bp, SC_CONV - 1, SC_WIDTH), F32)
    p_lists = [[] for _ in range(9)]
    s_lists = [[] for _ in range(9)]
    keep_p = min(WINDOW, lp)
    keep_s = min(WINDOW, win_buf + ls)

    def kv4(a, b, n):
        if a.ndim == 3:
            return a.reshape(b, ATT_KV_HEADS, HEAD_DIM, n).transpose(0, 3, 1, 2)
        return a.reshape(b, n, ATT_KV_HEADS, HEAD_DIM)

    def dims_major(c):
        return c.transpose(0, 1, 3, 4, 2).reshape(c.shape[0], c.shape[1], KV_WIDTH, c.shape[2])

    pages_kc, pages_vc, pages_ks, pages_vs = (dims_major(c) for c in (cache_k_cmp, cache_v_cmp, cache_k_sel, cache_v_sel))
    win_k, win_v = dims_major(cache_k_win), dims_major(cache_v_win)

    for l in range(depth):
        lw = _prep_layer(l, norm1_w, w_in, ssd_conv_w, ssd_conv_b, ssd_dt_bias, ssd_a_log, ssd_d, ssd_norm_w,
                         sc_conv_w, cmp_pe, cmp_w1, cmp_w2, w_out, norm2_w, w_ff1, w_ff2)
        final = l == depth - 1

        def prompt_attn(qrope, qraw, small, rows, lw=lw):
            ck, cvt = _compress_prompt(rows["kcmp_rows"], rows["vcmp_rows"], lw, batch=bp, seq_len=lp)
            return _attn_prompt(qrope, qraw, small, ck, cvt, rows["kselb"], rows["vselb"], rows["kwinb"], rows["vwinb"],
                                batch=bp, seq_len=lp)

        def sample_attn(qrope, qraw, small, rows, lw=lw, l=l):
            ck, cv = _compress_paged(page_table, pages_kc, pages_vc, lw, layer=l)
            return _attn_sample(page_table, pages_ks, pages_vs, win_k, win_v, qrope, qraw, small, ck, cv,
                                rows["kselb"], rows["vselb"], rows["kwinb"], rows["vwinb"], layer=l, seq_len=ls)

        hp, st_p, rows_p = _run_layer(hp, lw, invf, ssm0_p, cssd0_p, csc0_p, prompt_attn, fnw,
                                      batch=bp, seq_len=lp, pos0=0, final=final)
        hs, st_s, rows_s = _run_layer(hs, lw, invf, state_ssm[l], state_ssd_conv[l], state_sc_conv[l], sample_attn, fnw,
                                      batch=db, seq_len=ls, pos0=past, final=final)
        p_new = [kv4(rows_p[n], bp, lp) for n in ("kcmp", "vcmp", "ksel", "vsel")]
        p_new += [kv4(rows_p[n], bp, lp)[:, -keep_p:] for n in ("kwin", "vwin")]
        p_new += list(st_p)
        s_new = [kv4(rows_s[n], db, ls) for n in ("kcmp", "vcmp", "ksel", "vsel")]
        s_new += [jnp.concatenate([cache_k_win[l], kv4(rows_s["kwin"], db, ls)], axis=1)[:, -keep_s:],
                  jnp.concatenate([cache_v_win[l], kv4(rows_s["vwin"], db, ls)], axis=1)[:, -keep_s:]]
        s_new += list(st_s)
        for lst, arr in zip(p_lists, p_new):
            lst.append(arr)
        for lst, arr in zip(s_lists, s_new):
            lst.append(arr)

    p_out = [jnp.stack(a) for a in p_lists]
    s_out = [jnp.stack(a) for a in s_lists]
    y_prompt = hp.reshape(bp, lp, D_MODEL)
    y_sample = hs.reshape(db, ls, D_MODEL)
    return (y_prompt, y_sample, *p_out, *s_out)
```

```python
import functools
import math

import jax
import jax.numpy as jnp
from jax import lax
from jax.experimental import pallas as pl
from jax.experimental.pallas import tpu as pltpu

F32 = jnp.float32
BF16 = jnp.bfloat16
HIGHEST = lax.Precision.HIGHEST

D_MODEL = 1024
HEAD_DIM = 64
SSD_WIDTH = 384
SSD_HEADS = 6
SSD_GROUPS = 2
SSD_STATE = 128
SSD_CONV = 4
SSD_CHUNK = 128
SSD_XBC = SSD_WIDTH + 2 * SSD_GROUPS * SSD_STATE
SC_WIDTH = 256
SC_CONV = 3
ATT_WIDTH = 384
ATT_HEADS = 6
ATT_KV_HEADS = 2
GQA = ATT_HEADS // ATT_KV_HEADS
KV_WIDTH = ATT_KV_HEADS * HEAD_DIM
CMP_LEN = 32
CMP_STRIDE = 16
SEL_BLOCK = 64
N_SEL = 16
WINDOW = 512
PAGE_SIZE = 128
ROPE_THETA = 500000.0
ROT_DIM = HEAD_DIM // 4
D_FF = 4 * D_MODEL
EPS = 1e-6
NEG = -1e30
FORCE = 1e4
IN_SIZES = (SSD_WIDTH, SSD_XBC, SSD_HEADS, SC_WIDTH, SC_WIDTH, SC_WIDTH, ATT_WIDTH,
            KV_WIDTH, KV_WIDTH, KV_WIDTH, KV_WIDTH, KV_WIDTH, KV_WIDTH, ATT_HEADS * 3)

LANES = 128
SUBLANES = 8
VMEM_LIMIT_BYTES = 56 * 1024 * 1024

COL_Z = 0
COL_XBC = COL_Z + SSD_WIDTH
COL_SC = COL_XBC + SSD_XBC
COL_Q = COL_SC + 3 * SC_WIDTH
COL_KV = COL_Q + ATT_WIDTH
COL_SMALL = COL_KV + 6 * KV_WIDTH
IN_PAD = COL_SMALL + LANES
GATE_LANE0 = SSD_HEADS
KEYS_PER_TILE = 512
ATTN_ROWS = 256
INPROJ_ROWS = 512
FFN_ROWS = 512


def _dot(a, b):
    return jnp.dot(a, b, preferred_element_type=F32)


def _dot_nt(a, b):
    return lax.dot_general(a, b, (((1,), (1,)), ((), ())), preferred_element_type=F32)


def _dot_exact(a, b):
    return jnp.dot(a, b, preferred_element_type=F32, precision=HIGHEST)


def _silu(x):
    return x * (1.0 / (1.0 + jnp.exp(-x)))


def _sigmoid(x):
    return 1.0 / (1.0 + jnp.exp(-x))


def _rms(x, w):
    return x * lax.rsqrt(jnp.mean(x * x, axis=-1, keepdims=True) + EPS) * w


def _iota(shape, dim):
    return lax.broadcasted_iota(jnp.int32, shape, dim)


def _rope_chunk(x, cos_t, sin_t, ll):
    fwd = pltpu.roll(x, LANES - ROT_DIM // 2, 1)
    bwd = pltpu.roll(x, ROT_DIM // 2, 1)
    partner = jnp.where(ll < ROT_DIM // 2, fwd, bwd)
    return x * cos_t + partner * sin_t


def _inproj_kernel(x_ref, nw_ref, w_ref, invf_ref,
                   z_ref, xbc_ref, sc_ref, qraw_ref, qrope_ref,
                   kcmp_ref, vcmp_ref, ksel_ref, vsel_ref, kwin_ref, vwin_ref, small_ref,
                   kcmp_rows_ref, vcmp_rows_ref, kselb_ref, vselb_ref, kwinb_ref, vwinb_ref,
                   y_scr, *, tm, seq_len, pos0, dims_major):
    i = pl.program_id(0)
    y_scr[...] = _dot(_rms(x_ref[...], nw_ref[...]).astype(BF16), w_ref[...])

    def proj(c0, width):
        return y_scr[:, c0:c0 + width]

    z_ref[...] = proj(COL_Z, SSD_WIDTH)
    xbc_ref[...] = proj(COL_XBC, SSD_XBC)
    sc_ref[...] = proj(COL_SC, 3 * SC_WIDTH)

    row = i * tm + _iota((tm, LANES), 0)
    pos = (pos0 + (row & (seq_len - 1))).astype(F32)
    lane = _iota((tm, LANES), 1)
    ll = lane & (HEAD_DIM - 1)
    ang = pos * invf_ref[...]
    cos_a = jnp.cos(ang)
    sin_a = jnp.sin(ang)
    half = ROT_DIM // 2
    cos_t = jnp.where(ll < ROT_DIM, cos_a, 1.0)
    sin_t = jnp.where(ll < half, -sin_a, jnp.where(ll < ROT_DIM, sin_a, 0.0))

    q = proj(COL_Q, ATT_WIDTH)
    qraw_ref[...] = q.astype(BF16)
    qrope_ref[...] = jnp.concatenate(
        [_rope_chunk(q[:, c * LANES:(c + 1) * LANES], cos_t, sin_t, ll) for c in range(ATT_WIDTH // LANES)],
        axis=1).astype(BF16)

    kv = proj(COL_KV, 6 * KV_WIDTH)
    kcmp = kv[:, 0:LANES]
    vcmp = kv[:, LANES:2 * LANES]
    ksel = _rope_chunk(kv[:, 2 * LANES:3 * LANES], cos_t, sin_t, ll)
    vsel = kv[:, 3 * LANES:4 * LANES]
    kwin = _rope_chunk(kv[:, 4 * LANES:5 * LANES], cos_t, sin_t, ll)
    vwin = kv[:, 5 * LANES:6 * LANES]
    for ref, val in ((kcmp_ref, kcmp), (vcmp_ref, vcmp), (ksel_ref, ksel), (vsel_ref, vsel),
                     (kwin_ref, kwin), (vwin_ref, vwin)):
        if dims_major:
            ref[0] = val.T
        else:
            ref[...] = val
    kcmp_rows_ref[...] = kcmp
    vcmp_rows_ref[...] = vcmp
    kselb_ref[...] = ksel.astype(BF16)
    vselb_ref[...] = vsel.astype(BF16)
    kwinb_ref[...] = kwin.astype(BF16)
    vwinb_ref[...] = vwin.astype(BF16)

    sm = proj(COL_SMALL, LANES)
    is_gate = (lane >= GATE_LANE0) & (lane < GATE_LANE0 + 3 * ATT_HEADS)
    small_ref[...] = jnp.where(is_gate, _sigmoid(sm), sm)


def _inproj(h, norm_w, w_perm, invf, *, batch, seq_len, pos0):
    t = h.shape[0]
    tm = min(INPROJ_ROWS, t)
    assert t % tm == 0 and seq_len & (seq_len - 1) == 0
    dims_major = seq_len % tm == 0
    tiles_per_row = max(seq_len // tm, 1)
    row_spec = lambda w: pl.BlockSpec((tm, w), lambda i: (i, 0))
    const_spec = lambda a: pl.BlockSpec(a.shape, lambda i: (0, 0))
    row_out = lambda w, dt: (row_spec(w), jax.ShapeDtypeStruct((t, w), dt))
    if dims_major:
        kv_out = (pl.BlockSpec((1, KV_WIDTH, tm), lambda i: (i // tiles_per_row, 0, i % tiles_per_row)),
                  jax.ShapeDtypeStruct((batch, KV_WIDTH, seq_len), F32))
    else:
        kv_out = row_out(KV_WIDTH, F32)
    outs = ([row_out(w, F32) for w in (SSD_WIDTH, SSD_XBC, 3 * SC_WIDTH)] + [row_out(ATT_WIDTH, BF16)] * 2
            + [kv_out] * 6 + [row_out(LANES, F32)] + [row_out(KV_WIDTH, F32)] * 2 + [row_out(KV_WIDTH, BF16)] * 4)
    return pl.pallas_call(
        functools.partial(_inproj_kernel, tm=tm, seq_len=seq_len, pos0=pos0, dims_major=dims_major),
        grid=(t // tm,),
        in_specs=[row_spec(D_MODEL), const_spec(norm_w), const_spec(w_perm), const_spec(invf)],
        out_specs=[o[0] for o in outs],
        out_shape=[o[1] for o in outs],
        scratch_shapes=[pltpu.VMEM((tm, IN_PAD), F32)],
        compiler_params=pltpu.CompilerParams(dimension_semantics=("arbitrary",), vmem_limit_bytes=VMEM_LIMIT_BYTES),
        name="inproj",
    )(h, norm_w, w_perm, invf)


def _pad_rows(x, rows):
    if x.shape[0] == rows:
        return x
    return jnp.concatenate([x, jnp.zeros((rows - x.shape[0], x.shape[1]), x.dtype)], axis=0)


def _mixer_kernel(z_ref, xbc_ref, sc_ref, small_ref, ssm0_ref, cssd0_ref, csc0_ref,
                  cw_ref, cb_ref, dtb_ref, alog_ref, dskip_ref, nw_ref, scw_ref, tril_ref, expand_ref,
                  yssd_ref, ysc_ref, ssm_out_ref, cssd_out_ref, csc_out_ref,
                  s_scr, xp_scr, scp_scr, *, cs, n_chunks):
    C = SSD_CHUNK
    c = pl.program_id(1)

    @pl.when(c == 0)
    def _():
        s_scr[...] = ssm0_ref[0]
        xp_scr[0:SUBLANES, :] = cssd0_ref[0]
        scp_scr[0:SUBLANES, :] = csc0_ref[0]

    xp_scr[SUBLANES:SUBLANES + C, :] = _pad_rows(xbc_ref[...], C)
    cw = cw_ref[...]
    conv = cb_ref[...]
    for k in range(SSD_CONV):
        off = SUBLANES - (SSD_CONV - 1) + k
        conv = conv + cw[k:k + 1, :] * xp_scr[off:off + C, :]
    xbc_c = _silu(conv)
    tail_ssd = xp_scr[cs:cs + SUBLANES, :]
    xp_scr[0:SUBLANES, :] = tail_ssd

    xs = xbc_c[:, 0:SSD_WIDTH]
    bm = xbc_c[:, SSD_WIDTH:SSD_WIDTH + SSD_GROUPS * SSD_STATE]
    cm = xbc_c[:, SSD_WIDTH + SSD_GROUPS * SSD_STATE:SSD_XBC]

    lane = _iota((C, LANES), 1)
    rowi = _iota((C, LANES), 0)
    small = _pad_rows(small_ref[...], C)
    dt_raw = small + dtb_ref[...]
    dt = jnp.maximum(dt_raw, 0.0) + jnp.log1p(jnp.exp(-jnp.abs(dt_raw)))
    dt = jnp.where((lane < SSD_HEADS) & (rowi < cs), dt, 0.0)
    a_row = jnp.where(lane[0:1, :] < SSD_HEADS, -jnp.exp(alog_ref[...]), 0.0)
    acum = _dot_exact(tril_ref[...], dt * a_row)
    acum_t = acum.T
    a_last = acum[C - 1:C, :]
    expand = expand_ref[...]
    dt_w = _dot_exact(dt, expand)
    eacc_w = _dot_exact(jnp.exp(acum), expand)
    dend_w = _dot_exact(jnp.exp(a_last - acum), expand)
    cdec = jnp.exp(a_last)
    xdt = xs * dt_w
    xdtw_t = (xdt * dend_w).T

    causal = _iota((C, C), 0) >= _iota((C, C), 1)
    head_lo = lane < HEAD_DIM
    zeros_half = jnp.zeros((HEAD_DIM, SSD_STATE), BF16)
    y_chunks = [None] * (SSD_WIDTH // LANES)
    for g in range(SSD_GROUPS):
        bm_g = bm[:, g * SSD_STATE:(g + 1) * SSD_STATE].astype(BF16)
        cm_g = cm[:, g * SSD_STATE:(g + 1) * SSD_STATE].astype(BF16)
        cb = _dot_nt(cm_g, bm_g)
        for r in range(SSD_HEADS // SSD_GROUPS):
            h = g * (SSD_HEADS // SSD_GROUPS) + r
            ch, lo = h // 2, h % 2 == 0
            keep = head_lo if lo else jnp.logical_not(head_lo)
            seg = acum[:, h:h + 1] - acum_t[h:h + 1, :]
            decay = jnp.where(causal, jnp.exp(jnp.where(causal, seg, 0.0)), 0.0)
            xdt_h = jnp.where(keep, xdt[:, ch * LANES:(ch + 1) * LANES], 0.0).astype(BF16)
            y_d = _dot((cb * decay).astype(BF16), xdt_h)
            s_h = s_scr[h]
            s_b = s_h.astype(BF16)
            s_pl = jnp.concatenate([s_b, zeros_half] if lo else [zeros_half, s_b], axis=0)
            y_o = _dot_nt(cm_g, s_pl)
            y_h = y_d + eacc_w[:, ch * LANES:(ch + 1) * LANES] * y_o
            y_chunks[ch] = y_h if y_chunks[ch] is None else y_chunks[ch] + y_h
            s_scr[h] = s_h * cdec[:, h:h + 1] + _dot(xdtw_t[h * HEAD_DIM:(h + 1) * HEAD_DIM, :].astype(BF16), bm_g)
    y = jnp.concatenate(y_chunks, axis=1) + dskip_ref[...] * xs
    gated = y * _silu(_pad_rows(z_ref[...], C))
    yssd_ref[...] = _rms(gated, nw_ref[...])[0:cs]

    sc = _pad_rows(sc_ref[...], C)
    scp_scr[SUBLANES:SUBLANES + C, :] = sc[:, SC_WIDTH:2 * SC_WIDTH] * sc[:, 2 * SC_WIDTH:3 * SC_WIDTH]
    scw = scw_ref[...]
    conv3 = jnp.zeros((C, SC_WIDTH), F32)
    for k in range(SC_CONV):
        off = SUBLANES - (SC_CONV - 1) + k
        conv3 = conv3 + scw[k:k + 1, :] * scp_scr[off:off + C, :]
    ysc_ref[...] = (sc[:, 0:SC_WIDTH] * conv3)[0:cs]
    tail_sc = scp_scr[cs:cs + SUBLANES, :]
    scp_scr[0:SUBLANES, :] = tail_sc

    @pl.when(c == n_chunks - 1)
    def _():
        ssm_out_ref[0] = s_scr[...]
        cssd_out_ref[0] = tail_ssd
        csc_out_ref[0] = tail_sc


def _mixer(z, xbc, sc, small, ssm0, cssd0, csc0, lw, *, batch, seq_len):
    cs = min(SSD_CHUNK, seq_len)
    assert seq_len % cs == 0 and cs % SUBLANES == 0
    nc = seq_len // cs
    t = batch * seq_len
    row_spec = lambda w: pl.BlockSpec((cs, w), lambda b, c: (b * nc + c, 0))
    const_spec = lambda a: pl.BlockSpec(a.shape, lambda b, c: (0,) * a.ndim)
    batch_spec = lambda a: pl.BlockSpec((1,) + a.shape[1:], lambda b, c: (b,) + (0,) * (a.ndim - 1))
    consts = [lw["ssd_conv_w"], lw["ssd_conv_b"], lw["dt_bias"], lw["a_log"], lw["d_skip"], lw["ssd_norm_w"],
              lw["sc_conv_w"], lw["tril"], lw["expand"]]
    out_shape = [jax.ShapeDtypeStruct((t, SSD_WIDTH), F32), jax.ShapeDtypeStruct((t, SC_WIDTH), F32),
                 jax.ShapeDtypeStruct(ssm0.shape, F32), jax.ShapeDtypeStruct(cssd0.shape, F32),
                 jax.ShapeDtypeStruct(csc0.shape, F32)]
    return pl.pallas_call(
        functools.partial(_mixer_kernel, cs=cs, n_chunks=nc),
        grid=(batch, nc),
        in_specs=[row_spec(SSD_WIDTH), row_spec(SSD_XBC), row_spec(3 * SC_WIDTH), row_spec(LANES),
                  batch_spec(ssm0), batch_spec(cssd0), batch_spec(csc0)] + [const_spec(a) for a in consts],
        out_specs=[row_spec(SSD_WIDTH), row_spec(SC_WIDTH), batch_spec(ssm0), batch_spec(cssd0), batch_spec(csc0)],
        out_shape=out_shape,
        scratch_shapes=[pltpu.VMEM((SSD_HEADS, HEAD_DIM, SSD_STATE), F32),
                        pltpu.VMEM((SUBLANES + SSD_CHUNK, SSD_XBC), F32),
                        pltpu.VMEM((SUBLANES + SSD_CHUNK, SC_WIDTH), F32)],
        compiler_params=pltpu.CompilerParams(dimension_semantics=("arbitrary", "arbitrary"),
                                             vmem_limit_bytes=VMEM_LIMIT_BYTES),
        name="mixer",
    )(z, xbc, sc, small, ssm0, cssd0, csc0, *consts)


def _compress_units(units, pe0, pe1, w_first, w_second):
    n_units = units.shape[0]
    first = _dot((units + pe0).astype(BF16), w_first)
    second = _dot((units + pe1).astype(BF16), w_second)
    return _silu(first + pltpu.roll(second, n_units - 1, 0))


def _compress_prompt_kernel(kc_ref, vc_ref, pe_ref, wf_ref, ws_ref, w2_ref, w2t_ref, ck_ref, cvt_ref):
    hid_k = _compress_units(kc_ref[0], pe_ref[0, 0:1, :], pe_ref[0, 1:2, :], wf_ref[0], ws_ref[0])
    ck_ref[0] = _dot(hid_k.astype(BF16), w2_ref[0]).astype(BF16)
    hid_v = _compress_units(vc_ref[0], pe_ref[1, 0:1, :], pe_ref[1, 1:2, :], wf_ref[1], ws_ref[1])
    cvt_ref[0] = _dot(w2t_ref[1], hid_v.T.astype(BF16)).astype(BF16)


def _compress_prompt(kc, vc, lw, *, batch, seq_len):
    n_units = seq_len // CMP_STRIDE
    unit_w = CMP_STRIDE * KV_WIDTH
    kc = kc.reshape(batch, n_units, unit_w)
    vc = vc.reshape(batch, n_units, unit_w)
    consts = [lw["cmp_pe"], lw["cmp_wf"], lw["cmp_ws"], lw["cmp_w2"], lw["cmp_w2t"]]
    const_spec = lambda a: pl.BlockSpec(a.shape, lambda b: (0,) * a.ndim)
    in_spec = pl.BlockSpec((1, n_units, unit_w), lambda b: (b, 0, 0))
    return pl.pallas_call(
        _compress_prompt_kernel,
        grid=(batch,),
        in_specs=[in_spec, in_spec] + [const_spec(a) for a in consts],
        out_specs=[pl.BlockSpec((1, n_units, KV_WIDTH), lambda b: (b, 0, 0)),
                   pl.BlockSpec((1, KV_WIDTH, n_units), lambda b: (b, 0, 0))],
        out_shape=[jax.ShapeDtypeStruct((batch, n_units, KV_WIDTH), BF16),
                   jax.ShapeDtypeStruct((batch, KV_WIDTH, n_units), BF16)],
        compiler_params=pltpu.CompilerParams(dimension_semantics=("arbitrary",), vmem_limit_bytes=VMEM_LIMIT_BYTES),
        name="compress_prompt",
    )(kc, vc, *consts)


def _page_copy(cache_hbm, layer, page, buf, slot, p, sem):
    return pltpu.make_async_copy(cache_hbm.at[layer, page], buf.at[slot, p], sem)


def _pages_dims_major(buf, slot, first_page, n):
    return jnp.concatenate([buf[slot, first_page + p] for p in range(n)], axis=1)


def _gather_pages(pt_ref, b, cache_hbm, layer, buf, slot, sem, n_pages):
    def start(p, carry):
        _page_copy(cache_hbm, layer, pt_ref[b, p], buf, slot, p, sem).start()
        return carry
    lax.fori_loop(0, n_pages, start, 0)


def _wait_pages(cache_hbm, layer, buf, slot, sem, n_pages):
    def wait(p, carry):
        _page_copy(cache_hbm, layer, 0, buf, slot, p, sem).wait()
        return carry
    lax.fori_loop(0, n_pages, wait, 0)


def _prefetch_pages(pt_ref, cache_hbm, layer, buf, sems, n_pages):
    b = pl.program_id(0)
    slot = b % 2

    @pl.when(b == 0)
    def _():
        _gather_pages(pt_ref, 0, cache_hbm, layer, buf, 0, sems.at[0], n_pages)

    @pl.when(b + 1 < pl.num_programs(0))
    def _():
        _gather_pages(pt_ref, b + 1, cache_hbm, layer, buf, 1 - slot, sems.at[1 - slot], n_pages)
    return slot


def _compress_paged_kernel(pt_ref, kc_hbm, vc_hbm, pe_ref, wr_ref, w2_ref, ck_ref, cv_ref,
                           kbuf, vbuf, rows_scr, ksems, vsems, *, layer, n_pages):
    n_units = n_pages * (PAGE_SIZE // CMP_STRIDE)
    slot = _prefetch_pages(pt_ref, kc_hbm, layer, kbuf, ksems, n_pages)
    _prefetch_pages(pt_ref, vc_hbm, layer, vbuf, vsems, n_pages)
    for idx, (src_hbm, buf, sems, dst) in enumerate(((kc_hbm, kbuf, ksems, ck_ref), (vc_hbm, vbuf, vsems, cv_ref))):
        _wait_pages(src_hbm, layer, buf, slot, sems.at[slot], n_pages)

        def to_rows(c, carry, buf=buf):
            for p in range(pages_per_trip):
                page = c * pages_per_trip + p
                rows_scr[pl.ds(pl.multiple_of(page * PAGE_SIZE, PAGE_SIZE), PAGE_SIZE), :] = buf[slot, page].T
            return carry
        pages_per_trip = math.gcd(n_pages, 8)
        lax.fori_loop(0, n_pages // pages_per_trip, to_rows, 0)

        units = jnp.concatenate([rows_scr[pl.ds(r, n_units, stride=CMP_STRIDE), :].astype(BF16)
                                 for r in range(CMP_STRIDE)], axis=1)
        acc = _dot(units, wr_ref[idx])
        pe_rows = jnp.concatenate([jnp.broadcast_to(pe_ref[idx, h:h + 1, :], (SUBLANES // 2, pe_ref.shape[2]))
                                   for h in range(2)], axis=0)
        pe_both = _dot(pe_rows.astype(BF16), wr_ref[idx])
        pe_term = pe_both[0:1, 0:KV_WIDTH] + pe_both[SUBLANES // 2:SUBLANES // 2 + 1, KV_WIDTH:2 * KV_WIDTH]
        pre = acc[:, 0:KV_WIDTH] + pltpu.roll(acc[:, KV_WIDTH:2 * KV_WIDTH], n_units - 1, 0) + pe_term
        dst[0] = _dot(_silu(pre).astype(BF16), w2_ref[idx]).astype(BF16)


def _compress_paged(page_table, cache_kt, cache_vt, lw, *, layer):
    batch, n_pages = page_table.shape
    past = n_pages * PAGE_SIZE
    n_units = past // CMP_STRIDE
    consts = [lw["cmp_pe"], lw["cmp_wr"], lw["cmp_w2"]]
    const_spec = lambda a: pl.BlockSpec(a.shape, lambda b, pt: (0,) * a.ndim)
    any_spec = pl.BlockSpec(memory_space=pl.ANY)
    out_spec = pl.BlockSpec((1, n_units, KV_WIDTH), lambda b, pt: (b, 0, 0))
    grid_spec = pltpu.PrefetchScalarGridSpec(
        num_scalar_prefetch=1, grid=(batch,),
        in_specs=[any_spec, any_spec] + [const_spec(a) for a in consts],
        out_specs=[out_spec, out_spec],
        scratch_shapes=[pltpu.VMEM((2, n_pages, KV_WIDTH, PAGE_SIZE), F32), pltpu.VMEM((2, n_pages, KV_WIDTH, PAGE_SIZE), F32),
                        pltpu.VMEM((past, KV_WIDTH), F32),
                        pltpu.SemaphoreType.DMA((2,)), pltpu.SemaphoreType.DMA((2,))])
    return pl.pallas_call(
        functools.partial(_compress_paged_kernel, layer=layer, n_pages=n_pages),
        grid_spec=grid_spec,
        out_shape=[jax.ShapeDtypeStruct((batch, n_units, KV_WIDTH), BF16)] * 2,
        compiler_params=pltpu.CompilerParams(dimension_semantics=("arbitrary",), vmem_limit_bytes=VMEM_LIMIT_BYTES),
        name="compress_paged",
    )(page_table, cache_kt, cache_vt, *consts)


def _stack_heads(q):
    lane = _iota((q.shape[0], LANES), 1)
    lo = lane < HEAD_DIM
    c0, c1, c2 = (q[:, c * LANES:(c + 1) * LANES] for c in range(3))
    blocks = [jnp.where(lo, c0, 0.0), jnp.where(lo, pltpu.roll(c0, HEAD_DIM, 1), 0.0), jnp.where(lo, c1, 0.0),
              jnp.where(lo, 0.0, c1), jnp.where(lo, 0.0, pltpu.roll(c2, HEAD_DIM, 1)), jnp.where(lo, 0.0, c2)]
    return jnp.concatenate(blocks, axis=0)


def _unstack_heads(o, rows):
    lane = _iota((rows, LANES), 1)
    lo = lane < HEAD_DIM
    blk = [o[j * rows:(j + 1) * rows] for j in range(ATT_HEADS)]
    return jnp.concatenate([jnp.where(lo, blk[0], pltpu.roll(blk[1], HEAD_DIM, 1)),
                            jnp.where(lo, blk[2], blk[3]),
                            jnp.where(lo, pltpu.roll(blk[4], HEAD_DIM, 1), blk[5])], axis=1)


def _tile_heads(x, reps):
    return jnp.concatenate([x] * reps, axis=0)


def _masked_softmax(s, mask):
    s = jnp.where(mask, s, NEG)
    e = jnp.where(mask, jnp.exp(s - jnp.max(s, axis=1, keepdims=True)), 0.0)
    l = jnp.sum(e, axis=1, keepdims=True)
    return e * (1.0 / jnp.where(l > 0.0, l, 1.0))


def _split_bf16(x):
    hi = x.astype(BF16)
    return hi, (x - hi.astype(F32)).astype(BF16)


def _select_blocks(imp, tq, axis):
    n_blocks = imp.shape[axis]
    blk = _iota(imp.shape, axis)
    cur = tq >> 6
    causal = blk * SEL_BLOCK <= tq
    forced = causal & ((blk == 0) | (blk == cur) | (blk == cur - 1))
    score = jnp.where(forced, FORCE, jnp.where(causal, imp, -FORCE))
    blk_f = blk.astype(F32)
    sel = jnp.zeros(imp.shape, F32)
    for _ in range(N_SEL):
        mx = jnp.max(score, axis=axis, keepdims=True)
        first = jnp.min(jnp.where(score == mx, blk_f, float(n_blocks)), axis=axis, keepdims=True)
        pick = blk_f == first
        sel = jnp.where(pick & (mx > -FORCE / 2), 1.0, sel)
        score = jnp.where(pick, -jnp.inf, score)
    return sel


def _combine_gates(gates, o_c, o_s, o_w, rows):
    out = []
    for hh in range(ATT_HEADS):
        sl = slice(hh * rows, (hh + 1) * rows)
        g0 = GATE_LANE0 + 3 * hh
        out.append(gates[:, g0:g0 + 1] * o_c[sl] + gates[:, g0 + 1:g0 + 2] * o_s[sl] + gates[:, g0 + 2:g0 + 3] * o_w[sl])
    return jnp.concatenate(out, axis=0)


def _attn_prompt_kernel(qr_ref, qn_ref, small_ref, ck_ref, cvt_ref, ks_ref, vs_ref, kw_ref, vw_ref,
                        onehot_ref, mimp_ref, out_ref,
                        qaug_scr, sa_scr, sb_scr, p_scr, m_scr, l_scr, a_scr, acc_scr,
                        sct_scr, pct_scr, psum_scr, sw_scr, pw_scr, lw_scr, oc_scr, ow_scr):
    R = ATTN_ROWS
    TK = KEYS_PER_TILE
    i = pl.program_id(1)
    t0 = pl.multiple_of(i * R, R)
    scale = HEAD_DIM ** -0.5 * math.log2(math.e)
    tq = t0 + _iota((R, 1), 0)
    tq_row = t0 + _iota((1, R), 1)
    q_rope = _stack_heads(qr_ref[...].astype(F32) * scale).astype(BF16)
    q_raw = _stack_heads(qn_ref[...].astype(F32) * scale).astype(BF16)
    qaug_scr[:, 0:LANES] = q_rope

    span = WINDOW + R
    s0 = pl.multiple_of(jnp.maximum(t0 - WINDOW, 0), R)
    sw_scr[...] = _dot_nt(q_rope, kw_ref[pl.ds(s0, span), :])
    spos = s0 + _iota((1, span), 1)
    win_bias = jnp.where((spos <= tq) & (spos > tq - WINDOW), 0.0, NEG)
    for j in range(ATT_HEADS):
        rows = slice(j * R, (j + 1) * R)
        s = sw_scr[rows, :] + win_bias
        e = jnp.exp2(s - jnp.max(s, axis=1, keepdims=True))
        pw_scr[rows, :] = e.astype(BF16)
        lw_scr[rows, :] = jnp.broadcast_to(jnp.sum(e, axis=1, keepdims=True), (R, LANES))
    ow_scr[...] = _dot(pw_scr[...], vw_ref[pl.ds(s0, span), :]) * (1.0 / lw_scr[...])

    n_units = ck_ref.shape[1]
    sct_scr[...] = _dot_nt(ck_ref[0], q_raw)
    cmp_end = _iota((n_units, 1), 0) * CMP_STRIDE + (CMP_LEN - 1)
    cmp_bias = jnp.where(cmp_end <= tq_row, 0.0, NEG)
    for j in range(ATT_HEADS):
        cols = slice(j * R, (j + 1) * R)
        s = sct_scr[:, cols] + cmp_bias
        mx = jnp.max(s, axis=0, keepdims=True)
        e = jnp.exp2(s - mx)
        inv = jnp.where(mx > NEG / 2, 1.0 / jnp.sum(e, axis=0, keepdims=True), 0.0)
        p = e * inv
        pct_scr[:, cols] = p.astype(BF16)
        k, g = divmod(j, GQA)
        if g == 0:
            psum_scr[k] = p
        else:
            psum_scr[k] += p
    o_c_t = _dot(cvt_ref[0], pct_scr[...])
    oc_scr[...] = jnp.concatenate([o_c_t[:, j * R:(j + 1) * R].T for j in range(ATT_HEADS)], axis=0)

    for k in range(ATT_KV_HEADS):
        hi, lo = _split_bf16(psum_scr[k])
        imp_t = _dot(mimp_ref[...], hi) + _dot(mimp_ref[...], lo)
        sel_t = _select_blocks(imp_t, tq_row, 0)
        bias = jnp.where(sel_t.T > 0.5, 0.0, NEG).astype(BF16)
        for g in range(GQA):
            j = k * GQA + g
            qaug_scr[j * R:(j + 1) * R, LANES:2 * LANES] = bias

    def scores(k0, s_ref):
        k_aug = jnp.concatenate([ks_ref[pl.ds(k0, TK), :], onehot_ref[pl.ds(k0, TK), :]], axis=1)
        s_ref[...] = _dot_nt(qaug_scr[...], k_aug)

    def accumulate(k0, s_ref, first):
        if first:
            causal_bias = jnp.where((k0 + _iota((1, TK), 1)) <= tq, 0.0, NEG)
        for j in range(ATT_HEADS):
            rows = slice(j * R, (j + 1) * R)
            s = s_ref[rows, :]
            if first:
                s = s + causal_bias
                m_new = jnp.broadcast_to(jnp.max(s, axis=1, keepdims=True), (R, LANES))
            else:
                m_old = m_scr[rows, :]
                m_new = jnp.maximum(m_old, jnp.max(s, axis=1, keepdims=True))
                alpha = jnp.exp2(m_old - m_new)
            p = jnp.exp2(s - jnp.concatenate([m_new] * (TK // LANES), axis=1))
            l_new = jnp.sum(p, axis=1, keepdims=True)
            p_scr[rows, :] = p.astype(BF16)
            m_scr[rows, :] = m_new
            if first:
                l_scr[rows, :] = jnp.broadcast_to(l_new, (R, LANES))
            else:
                l_scr[rows, :] = alpha * l_scr[rows, :] + l_new
                a_scr[rows, :] = alpha
        pv = _dot(p_scr[...], vs_ref[pl.ds(k0, TK), :])
        if first:
            acc_scr[...] = pv
        else:
            acc_scr[...] = a_scr[...] * acc_scr[...] + pv

    n_past = t0 // TK
    k_diag = pl.multiple_of(n_past * TK, TK)
    scores(k_diag, sa_scr)
    accumulate(k_diag, sa_scr, True)

    @pl.when(n_past % 2 == 1)
    def _():
        k_odd = pl.multiple_of((n_past - 1) * TK, TK)
        scores(k_odd, sb_scr)
        accumulate(k_odd, sb_scr, False)

    scores(0, sa_scr)

    def pair_body(jj, carry):
        k_even = pl.multiple_of(2 * jj * TK, TK)
        k_odd = pl.multiple_of(k_even + TK, TK)
        scores(k_odd, sb_scr)
        accumulate(k_even, sa_scr, False)
        scores(pl.multiple_of(k_odd + TK, TK), sa_scr)
        accumulate(k_odd, sb_scr, False)
        return carry
    lax.fori_loop(0, n_past // 2, pair_body, 0)
    o_s = acc_scr[...] * (1.0 / l_scr[...])

    out_ref[...] = _unstack_heads(_combine_gates(small_ref[...], oc_scr[...], o_s, ow_scr[...], R), R)


def _attn_prompt(qrope, qraw, small, ck, cvt, kselb, vselb, kwinb, vwinb, *, batch, seq_len):
    span = WINDOW + ATTN_ROWS
    assert seq_len % KEYS_PER_TILE == 0 and seq_len >= span and seq_len // SEL_BLOCK <= LANES
    assert KEYS_PER_TILE % ATTN_ROWS == 0
    nqb = seq_len // ATTN_ROWS
    n_units = seq_len // CMP_STRIDE
    t = batch * seq_len
    pos = jnp.arange(seq_len, dtype=jnp.int32)
    onehot = (pos[:, None] // SEL_BLOCK == jnp.arange(LANES, dtype=jnp.int32)[None, :]).astype(BF16)
    m_imp_t = _importance_matrix(n_units, LANES).T
    q_spec = lambda w: pl.BlockSpec((ATTN_ROWS, w), lambda b, i: (b * nqb + i, 0))
    batch_spec = lambda a: pl.BlockSpec((1,) + a.shape[1:], lambda b, i: (b, 0, 0))
    seq_spec = pl.BlockSpec((seq_len, KV_WIDTH), lambda b, i: (b, 0))
    const_spec = lambda a: pl.BlockSpec(a.shape, lambda b, i: (0, 0))
    rows = ATT_HEADS * ATTN_ROWS
    vmem = pltpu.VMEM
    scratch = [vmem((rows, 2 * LANES), BF16),
               vmem((rows, KEYS_PER_TILE), F32), vmem((rows, KEYS_PER_TILE), F32),
               vmem((rows, KEYS_PER_TILE), BF16),
               vmem((rows, LANES), F32), vmem((rows, LANES), F32), vmem((rows, LANES), F32),
               vmem((rows, LANES), F32),
               vmem((n_units, rows), F32), vmem((n_units, rows), BF16),
               vmem((ATT_KV_HEADS, n_units, ATTN_ROWS), F32),
               vmem((rows, span), F32), vmem((rows, span), BF16), vmem((rows, LANES), F32),
               vmem((rows, LANES), F32), vmem((rows, LANES), F32)]
    return pl.pallas_call(
        _attn_prompt_kernel,
        grid=(batch, nqb),
        in_specs=[q_spec(ATT_WIDTH), q_spec(ATT_WIDTH), q_spec(LANES), batch_spec(ck), batch_spec(cvt),
                  seq_spec, seq_spec, seq_spec, seq_spec, const_spec(onehot), const_spec(m_imp_t)],
        out_specs=q_spec(ATT_WIDTH),
        out_shape=jax.ShapeDtypeStruct((t, ATT_WIDTH), F32),
        scratch_shapes=scratch,
        compiler_params=pltpu.CompilerParams(dimension_semantics=("arbitrary", "arbitrary"),
                                             vmem_limit_bytes=VMEM_LIMIT_BYTES),
        name="attn_prompt",
    )(qrope, qraw, small, ck, cvt, kselb, vselb, kwinb, vwinb, onehot, m_imp_t)


def _importance_matrix(n_units, width):
    n = jnp.arange(n_units, dtype=jnp.int32)[:, None]
    j = jnp.arange(width, dtype=jnp.int32)[None, :]
    upb = SEL_BLOCK // CMP_STRIDE
    m = (n // upb == j).astype(F32) + ((n + 1) // upb == j).astype(F32)
    return jnp.where(n < n_units - 1, m, 0.0).astype(BF16)


def _attn_sample_kernel(pt_ref, ks_hbm, vs_hbm, qr_ref, qn_ref, small_ref, ck_ref, cv_ref,
                        knew_ref, vnew_ref, kwc_ref, vwc_ref, kwnew_ref, vwnew_ref, onehot_ref, mimp_ref,
                        out_ref, kbuf, vbuf, ksems, vsems, *, layer, n_pages, past, win_buf, group_blocks):
    R = qr_ref.shape[0]
    slot = _prefetch_pages(pt_ref, ks_hbm, layer, kbuf, ksems, n_pages)
    _prefetch_pages(pt_ref, vs_hbm, layer, vbuf, vsems, n_pages)

    scale = HEAD_DIM ** -0.5
    tq = past + _iota((R, 1), 0)
    tq_all = _tile_heads(tq, ATT_HEADS)
    q_rope = _stack_heads(qr_ref[...].astype(F32) * scale).astype(BF16)
    q_raw = _stack_heads(qn_ref[...].astype(F32) * scale).astype(BF16)

    n_units = ck_ref.shape[1]
    s_c = _dot_nt(q_raw, ck_ref[0])
    cmp_end = _iota((1, n_units), 1) * CMP_STRIDE + (CMP_LEN - 1)
    p_c = _masked_softmax(s_c, cmp_end <= tq_all)
    o_c = _dot(p_c.astype(BF16), cv_ref[0])

    bias = []
    for k in range(ATT_KV_HEADS):
        hi, lo = _split_bf16(sum(p_c[(k * GQA + g) * R:(k * GQA + g + 1) * R] for g in range(GQA)))
        sel = _select_blocks(_dot(hi, mimp_ref[...]) + _dot(lo, mimp_ref[...]), tq, 1)
        bias.append(_tile_heads(jnp.where(sel > 0.5, 0.0, NEG), GQA))
    bias = jnp.concatenate(bias, axis=0).astype(BF16)

    k_off = past - win_buf
    s_wc = _dot(q_rope, kwc_ref[0, 0].astype(BF16))
    s_wn = _dot_nt(q_rope, kwnew_ref[...])
    pos_c = k_off + _iota((1, win_buf), 1)
    pos_n = past + _iota((1, R), 1)
    mask_c = (pos_c <= tq_all) & (pos_c > tq_all - WINDOW) & (pos_c >= k_off)
    mask_n = (pos_n <= tq_all) & (pos_n > tq_all - WINDOW) & (pos_n >= k_off)
    s_wc = jnp.where(mask_c, s_wc, NEG)
    s_wn = jnp.where(mask_n, s_wn, NEG)
    m_w = jnp.maximum(jnp.max(s_wc, axis=1, keepdims=True), jnp.max(s_wn, axis=1, keepdims=True))
    e_wc = jnp.where(mask_c, jnp.exp(s_wc - m_w), 0.0)
    e_wn = jnp.where(mask_n, jnp.exp(s_wn - m_w), 0.0)
    l_w = jnp.sum(e_wc, axis=1, keepdims=True) + jnp.sum(e_wn, axis=1, keepdims=True)
    o_w = _dot_nt(e_wc.astype(BF16), vwc_ref[0, 0].astype(BF16)) + _dot(e_wn.astype(BF16), vwnew_ref[...])
    o_w = o_w * (1.0 / jnp.where(l_w > 0.0, l_w, 1.0))

    s_n = _dot_nt(q_rope, knew_ref[...])
    mask_sn = pos_n <= tq_all
    s_n = jnp.where(mask_sn, s_n, NEG)
    _wait_pages(ks_hbm, layer, kbuf, slot, ksems.at[slot], n_pages)
    group_keys = group_blocks * SEL_BLOCK
    n_groups = (n_pages * PAGE_SIZE) // group_keys
    s_past = []
    group_pages = group_keys // PAGE_SIZE
    for gi in range(n_groups):
        q_aug = jnp.concatenate([q_rope, bias[:, gi * group_blocks:(gi + 1) * group_blocks]], axis=1)
        k_group = _pages_dims_major(kbuf, slot, gi * group_pages, group_pages)
        k_aug_t = jnp.concatenate([k_group.astype(BF16), onehot_ref[...]], axis=0)
        s_past.append(_dot(q_aug, k_aug_t))
    m_s = jnp.max(s_n, axis=1, keepdims=True)
    for s in s_past:
        m_s = jnp.maximum(m_s, jnp.max(s, axis=1, keepdims=True))
    e_n = jnp.where(mask_sn, jnp.exp(s_n - m_s), 0.0)
    l_s = jnp.sum(e_n, axis=1, keepdims=True)
    o_s = _dot(e_n.astype(BF16), vnew_ref[...])
    _wait_pages(vs_hbm, layer, vbuf, slot, vsems.at[slot], n_pages)
    for gi, s in enumerate(s_past):
        e = jnp.exp(s - m_s)
        l_s = l_s + jnp.sum(e, axis=1, keepdims=True)
        v_group = _pages_dims_major(vbuf, slot, gi * group_pages, group_pages)
        o_s = o_s + _dot_nt(e.astype(BF16), v_group.astype(BF16))
    o_s = o_s * (1.0 / l_s)

    out_ref[...] = _unstack_heads(_combine_gates(small_ref[...], o_c, o_s, o_w, R), R)


def _attn_sample(page_table, cache_kst, cache_vst, cache_kwt, cache_vwt, qrope, qraw, small, ck, cv,
                 knew, vnew, kwnew, vwnew, *, layer, seq_len):
    batch, n_pages = page_table.shape
    past = n_pages * PAGE_SIZE
    win_buf = cache_kwt.shape[3]
    assert seq_len == SUBLANES and (past + seq_len) // CMP_STRIDE == past // CMP_STRIDE and past >= WINDOW
    n_units = past // CMP_STRIDE
    nb_past = past // SEL_BLOCK
    group_blocks = min(LANES, nb_past)
    assert nb_past % group_blocks == 0
    n_blk_lanes = -(-(nb_past + 1) // LANES) * LANES
    key = jnp.arange(group_blocks * SEL_BLOCK, dtype=jnp.int32)
    onehot = (key[None, :] // SEL_BLOCK == jnp.arange(group_blocks, dtype=jnp.int32)[:, None]).astype(BF16)
    m_imp = _importance_matrix(n_units, n_blk_lanes)
    win_spec = pl.BlockSpec((1, 1, KV_WIDTH, win_buf), lambda b, pt: (layer, b, 0, 0))
    any_spec = pl.BlockSpec(memory_space=pl.ANY)
    row_spec = lambda w: pl.BlockSpec((seq_len, w), lambda b, pt: (b, 0))
    batch_spec = lambda a: pl.BlockSpec((1,) + a.shape[1:], lambda b, pt: (b, 0, 0))
    const_spec = lambda a: pl.BlockSpec(a.shape, lambda b, pt: (0, 0))
    grid_spec = pltpu.PrefetchScalarGridSpec(
        num_scalar_prefetch=1, grid=(batch,),
        in_specs=[any_spec, any_spec, row_spec(ATT_WIDTH), row_spec(ATT_WIDTH), row_spec(LANES),
                  batch_spec(ck), batch_spec(cv), row_spec(KV_WIDTH), row_spec(KV_WIDTH),
                  win_spec, win_spec, row_spec(KV_WIDTH), row_spec(KV_WIDTH),
                  const_spec(onehot), const_spec(m_imp)],
        out_specs=row_spec(ATT_WIDTH),
        scratch_shapes=[pltpu.VMEM((2, n_pages, KV_WIDTH, PAGE_SIZE), F32), pltpu.VMEM((2, n_pages, KV_WIDTH, PAGE_SIZE), F32),
                        pltpu.SemaphoreType.DMA((2,)), pltpu.SemaphoreType.DMA((2,))])
    return pl.pallas_call(
        functools.partial(_attn_sample_kernel, layer=layer, n_pages=n_pages, past=past, win_buf=win_buf,
                          group_blocks=group_blocks),
        grid_spec=grid_spec,
        out_shape=jax.ShapeDtypeStruct((batch * seq_len, ATT_WIDTH), F32),
        compiler_params=pltpu.CompilerParams(dimension_semantics=("arbitrary",), vmem_limit_bytes=VMEM_LIMIT_BYTES),
        name="attn_sample",
    )(page_table, cache_kst, cache_vst, qrope, qraw, small, ck, cv, knew, vnew, cache_kwt, cache_vwt, kwnew, vwnew,
      onehot, m_imp)


def _ffn_kernel(h_ref, yssd_ref, ysc_ref, yatt_ref, n2_ref, fn_ref, wo_hbm, w1_hbm, w2_hbm, out_ref,
                wo_scr, w1_scr, w2_scr, hid_scr, sems, *, final):
    @pl.when(pl.program_id(0) == 0)
    def _():
        copies = [pltpu.make_async_copy(src, dst, sems.at[n])
                  for n, (src, dst) in enumerate(((wo_hbm, wo_scr), (w1_hbm, w1_scr), (w2_hbm, w2_scr)))]
        for c in copies:
            c.start()
        for c in copies:
            c.wait()

    mix = (_dot(yssd_ref[...].astype(BF16), wo_scr[0:SSD_WIDTH, :])
           + _dot(ysc_ref[...].astype(BF16), wo_scr[SSD_WIDTH:SSD_WIDTH + SC_WIDTH, :])
           + _dot(yatt_ref[...].astype(BF16), wo_scr[SSD_WIDTH + SC_WIDTH:D_MODEL, :]))
    hn = h_ref[...] + mix
    xn = _rms(hn, n2_ref[...]).astype(BF16)
    hid_scr[...] = jnp.square(jnp.maximum(_dot(xn, w1_scr[...]), 0.0)).astype(BF16)
    o = hn + _dot(hid_scr[...], w2_scr[...])
    out_ref[...] = _rms(o, fn_ref[...]) if final else o


def _ffn(h, yssd, ysc, yatt, lw, final_norm_w, *, final):
    t = h.shape[0]
    tm = min(FFN_ROWS, t)
    assert t % tm == 0
    row_spec = lambda w: pl.BlockSpec((tm, w), lambda i: (i, 0))
    const_spec = lambda a: pl.BlockSpec(a.shape, lambda i: (0, 0))
    any_spec = pl.BlockSpec(memory_space=pl.ANY)
    return pl.pallas_call(
        functools.partial(_ffn_kernel, final=final),
        grid=(t // tm,),
        in_specs=[row_spec(D_MODEL), row_spec(SSD_WIDTH), row_spec(SC_WIDTH), row_spec(ATT_WIDTH),
                  const_spec(lw["norm2_w"]), const_spec(final_norm_w), any_spec, any_spec, any_spec],
        out_specs=row_spec(D_MODEL),
        out_shape=jax.ShapeDtypeStruct((t, D_MODEL), F32),
        scratch_shapes=[pltpu.VMEM((D_MODEL, D_MODEL), BF16), pltpu.VMEM((D_MODEL, D_FF), BF16),
                        pltpu.VMEM((D_FF, D_MODEL), BF16), pltpu.VMEM((tm, D_FF), BF16),
                        pltpu.SemaphoreType.DMA((3,))],
        compiler_params=pltpu.CompilerParams(dimension_semantics=("arbitrary",), vmem_limit_bytes=VMEM_LIMIT_BYTES),
        name="ffn",
    )(h, yssd, ysc, yatt, lw["norm2_w"], final_norm_w, lw["w_out"], lw["w_ff1"], lw["w_ff2"])


def _lane_row(v, width):
    v = v.reshape(1, -1).astype(F32)
    return jnp.pad(v, ((0, 0), (0, width - v.shape[1])))


def _prep_layer(l, norm1_w, w_in, ssd_conv_w, ssd_conv_b, ssd_dt_bias, ssd_a_log, ssd_d, ssd_norm_w, sc_conv_w,
                cmp_pe, cmp_w1, cmp_w2, w_out, norm2_w, w_ff1, w_ff2):
    offs = [0]
    for s in IN_SIZES:
        offs.append(offs[-1] + s)
    seg = lambda a: jnp.arange(offs[a], offs[a + 1])
    perm = jnp.concatenate([seg(0), seg(1), seg(3), seg(4), seg(5), seg(6)] + [seg(a) for a in range(7, 13)]
                           + [seg(2), seg(13)])
    w_perm = jnp.pad(w_in[l][:, perm], ((0, 0), (0, IN_PAD - offs[-1]))).astype(BF16)

    eye = jnp.eye(ATT_KV_HEADS, dtype=F32)
    w1 = cmp_w1[l].reshape(2, 2, CMP_STRIDE, HEAD_DIM, HEAD_DIM)
    wfs = jnp.einsum('vhrde,kK->vhrkdKe', w1, eye).reshape(2, 2, CMP_STRIDE * KV_WIDTH, KV_WIDTH).astype(BF16)
    pe = cmp_pe[l].reshape(2, 2, CMP_STRIDE, 1, HEAD_DIM)
    pe = jnp.broadcast_to(pe, (2, 2, CMP_STRIDE, ATT_KV_HEADS, HEAD_DIM)).reshape(2, 2, CMP_STRIDE * KV_WIDTH)
    w2 = jnp.einsum('vde,kK->vkdKe', cmp_w2[l], eye).reshape(2, KV_WIDTH, KV_WIDTH).astype(BF16)

    head = jnp.arange(LANES, dtype=jnp.int32)[:, None]
    col = jnp.arange(SSD_WIDTH, dtype=jnp.int32)[None, :]
    idx = jnp.arange(SSD_CHUNK, dtype=jnp.int32)
    return {
        "norm1_w": norm1_w[l].reshape(1, D_MODEL), "w_in": w_perm,
        "ssd_conv_w": ssd_conv_w[l], "ssd_conv_b": ssd_conv_b[l].reshape(1, SSD_XBC),
        "dt_bias": _lane_row(ssd_dt_bias[l], LANES), "a_log": _lane_row(ssd_a_log[l], LANES),
        "d_skip": jnp.repeat(ssd_d[l], HEAD_DIM).reshape(1, SSD_WIDTH),
        "ssd_norm_w": ssd_norm_w[l].reshape(1, SSD_WIDTH), "sc_conv_w": sc_conv_w[l],
        "tril": (idx[:, None] >= idx[None, :]).astype(F32),
        "expand": (col // HEAD_DIM == head).astype(F32),
        "cmp_pe": pe, "cmp_wf": wfs[:, 0], "cmp_ws": wfs[:, 1], "cmp_w2": w2,
        "cmp_wr": jnp.concatenate([wfs[:, 0], wfs[:, 1]], axis=-1), "cmp_w2t": w2.transpose(0, 2, 1),
        "w_out": w_out[l].astype(BF16), "norm2_w": norm2_w[l].reshape(1, D_MODEL),
        "w_ff1": w_ff1[l].astype(BF16), "w_ff2": w_ff2[l].astype(BF16),
    }


def _front_pad_rows(a, rows):
    return jnp.pad(a, ((0, 0), (rows - a.shape[1], 0), (0, 0)))


def _run_layer(h, lw, invf, ssm0, cssd0, csc0, attn_fn, final_norm_w, *, batch, seq_len, pos0, final):
    (z, xbc, sc, qraw, qrope, kcmp, vcmp, ksel, vsel, kwin, vwin, small, kcmp_rows, vcmp_rows,
     kselb, vselb, kwinb, vwinb) = _inproj(h, lw["norm1_w"], lw["w_in"], invf, batch=batch, seq_len=seq_len, pos0=pos0)
    yssd, ysc, ssm_new, cssd_new, csc_new = _mixer(
        z, xbc, sc, small, ssm0, _front_pad_rows(cssd0, SUBLANES), _front_pad_rows(csc0, SUBLANES), lw,
        batch=batch, seq_len=seq_len)
    rows = dict(kcmp=kcmp, vcmp=vcmp, ksel=ksel, vsel=vsel, kwin=kwin, vwin=vwin, kcmp_rows=kcmp_rows,
                vcmp_rows=vcmp_rows, kselb=kselb, vselb=vselb, kwinb=kwinb, vwinb=vwinb)
    yatt = attn_fn(qrope, qraw, small, rows)
    h = _ffn(h, yssd, ysc, yatt, lw, final_norm_w, final=final)
    states = (ssm_new, cssd_new[:, SUBLANES - (SSD_CONV - 1):], csc_new[:, SUBLANES - (SC_CONV - 1):])
    return h, states, rows


def kernel(x_prompt, x_sample, cache_k_cmp, cache_v_cmp, cache_k_sel, cache_v_sel, cache_k_win, cache_v_win, state_ssm, state_ssd_conv, state_sc_conv, page_table, norm1_w, w_in, ssd_conv_w, ssd_conv_b, ssd_dt_bias, ssd_a_log, ssd_d, ssd_norm_w, sc_conv_w, cmp_pe, cmp_w1, cmp_w2, w_out, norm2_w, w_ff1, w_ff2, final_norm_w):
    bp, lp, _ = x_prompt.shape
    db, ls, _ = x_sample.shape
    depth = w_in.shape[0]
    n_pages = page_table.shape[1]
    past = n_pages * PAGE_SIZE
    win_buf = cache_k_win.shape[2]
    fnw = final_norm_w.reshape(1, D_MODEL)

    half = ROT_DIM // 2
    inv_freq = ROPE_THETA ** (-jnp.arange(half, dtype=F32) * 2.0 / ROT_DIM)
    invf = jnp.tile(inv_freq, LANES // half).reshape(1, LANES)

    hp = x_prompt.reshape(bp * lp, D_MODEL)
    hs = x_sample.reshape(db * ls, D_MODEL)
    ssm0_p = jnp.zeros((bp, SSD_HEADS, HEAD_DIM, SSD_STATE), F32)
    cssd0_p = jnp.zeros((bp, SSD_CONV - 1, SSD_XBC), F32)
    csc0_p = jnp.zeros((bp, SC_CONV - 1, SC_WIDTH), F32)
    p_lists = [[] for _ in range(9)]
    s_lists = [[] for _ in range(9)]
    keep_p = min(WINDOW, lp)
    keep_s = min(WINDOW, win_buf + ls)

    def kv4(a, b, n):
        if a.ndim == 3:
            return a.reshape(b, ATT_KV_HEADS, HEAD_DIM, n).transpose(0, 3, 1, 2)
        return a.reshape(b, n, ATT_KV_HEADS, HEAD_DIM)

    def dims_major(c):
        return c.transpose(0, 1, 3, 4, 2).reshape(c.shape[0], c.shape[1], KV_WIDTH, c.shape[2])

    pages_kc, pages_vc, pages_ks, pages_vs = (dims_major(c) for c in (cache_k_cmp, cache_v_cmp, cache_k_sel, cache_v_sel))
    win_k, win_v = dims_major(cache_k_win), dims_major(cache_v_win)

    for l in range(depth):
        lw = _prep_layer(l, norm1_w, w_in, ssd_conv_w, ssd_conv_b, ssd_dt_bias, ssd_a_log, ssd_d, ssd_norm_w,
                         sc_conv_w, cmp_pe, cmp_w1, cmp_w2, w_out, norm2_w, w_ff1, w_ff2)
        final = l == depth - 1

        def prompt_attn(qrope, qraw, small, rows, lw=lw):
            ck, cvt = _compress_prompt(rows["kcmp_rows"], rows["vcmp_rows"], lw, batch=bp, seq_len=lp)
            return _attn_prompt(qrope, qraw, small, ck, cvt, rows["kselb"], rows["vselb"], rows["kwinb"], rows["vwinb"],
                                batch=bp, seq_len=lp)

        def sample_attn(qrope, qraw, small, rows, lw=lw, l=l):
            ck, cv = _compress_paged(page_table, pages_kc, pages_vc, lw, layer=l)
            return _attn_sample(page_table, pages_ks, pages_vs, win_k, win_v, qrope, qraw, small, ck, cv,
                                rows["kselb"], rows["vselb"], rows["kwinb"], rows["vwinb"], layer=l, seq_len=ls)

        hp, st_p, rows_p = _run_layer(hp, lw, invf, ssm0_p, cssd0_p, csc0_p, prompt_attn, fnw,
                                      batch=bp, seq_len=lp, pos0=0, final=final)
        hs, st_s, rows_s = _run_layer(hs, lw, invf, state_ssm[l], state_ssd_conv[l], state_sc_conv[l], sample_attn, fnw,
                                      batch=db, seq_len=ls, pos0=past, final=final)
        p_new = [kv4(rows_p[n], bp, lp) for n in ("kcmp", "vcmp", "ksel", "vsel")]
        p_new += [kv4(rows_p[n], bp, lp)[:, -keep_p:] for n in ("kwin", "vwin")]
        p_new += list(st_p)
        s_new = [kv4(rows_s[n], db, ls) for n in ("kcmp", "vcmp", "ksel", "vsel")]
        s_new += [jnp.concatenate([cache_k_win[l], kv4(rows_s["kwin"], db, ls)], axis=1)[:, -keep_s:],
                  jnp.concatenate([cache_v_win[l], kv4(rows_s["vwin"], db, ls)], axis=1)[:, -keep_s:]]
        s_new += list(st_s)
        for lst, arr in zip(p_lists, p_new):
            lst.append(arr)
        for lst, arr in zip(s_lists, s_new):
            lst.append(arr)

    p_out = [jnp.stack(a) for a in p_lists]
    s_out = [jnp.stack(a) for a in s_lists]
    y_prompt = hp.reshape(bp, lp, D_MODEL)
    y_sample = hs.reshape(db, ls, D_MODEL)
    return (y_prompt, y_sample, *p_out, *s_out)
```

```python
import functools
import math

import jax
import jax.numpy as jnp
from jax import lax
from jax.experimental import pallas as pl
from jax.experimental.pallas import tpu as pltpu

F32 = jnp.float32
BF16 = jnp.bfloat16
HIGHEST = lax.Precision.HIGHEST

D_MODEL = 1024
HEAD_DIM = 64
SSD_WIDTH = 384
SSD_HEADS = 6
SSD_GROUPS = 2
SSD_STATE = 128
SSD_CONV = 4
SSD_CHUNK = 128
SSD_XBC = SSD_WIDTH + 2 * SSD_GROUPS * SSD_STATE
SC_WIDTH = 256
SC_CONV = 3
ATT_WIDTH = 384
ATT_HEADS = 6
ATT_KV_HEADS = 2
GQA = ATT_HEADS // ATT_KV_HEADS
KV_WIDTH = ATT_KV_HEADS * HEAD_DIM
CMP_LEN = 32
CMP_STRIDE = 16
SEL_BLOCK = 64
N_SEL = 16
WINDOW = 512
PAGE_SIZE = 128
ROPE_THETA = 500000.0
ROT_DIM = HEAD_DIM // 4
D_FF = 4 * D_MODEL
EPS = 1e-6
NEG = -1e30
FORCE = 1e4
IN_SIZES = (SSD_WIDTH, SSD_XBC, SSD_HEADS, SC_WIDTH, SC_WIDTH, SC_WIDTH, ATT_WIDTH,
            KV_WIDTH, KV_WIDTH, KV_WIDTH, KV_WIDTH, KV_WIDTH, KV_WIDTH, ATT_HEADS * 3)

LANES = 128
SUBLANES = 8
VMEM_LIMIT_BYTES = 56 * 1024 * 1024

COL_Z = 0
COL_XBC = COL_Z + SSD_WIDTH
COL_SC = COL_XBC + SSD_XBC
COL_Q = COL_SC + 3 * SC_WIDTH
COL_KV = COL_Q + ATT_WIDTH
COL_SMALL = COL_KV + 6 * KV_WIDTH
IN_PAD = COL_SMALL + LANES
GATE_LANE0 = SSD_HEADS
KEYS_PER_TILE = 512
ATTN_ROWS = 256
INPROJ_ROWS = 512
FFN_ROWS = 512


def _dot(a, b):
    return jnp.dot(a, b, preferred_element_type=F32)


def _dot_nt(a, b):
    return lax.dot_general(a, b, (((1,), (1,)), ((), ())), preferred_element_type=F32)


def _dot_exact(a, b):
    return jnp.dot(a, b, preferred_element_type=F32, precision=HIGHEST)


def _silu(x):
    return x * (1.0 / (1.0 + jnp.exp(-x)))


def _sigmoid(x):
    return 1.0 / (1.0 + jnp.exp(-x))


def _rms(x, w):
    return x * lax.rsqrt(jnp.mean(x * x, axis=-1, keepdims=True) + EPS) * w


def _iota(shape, dim):
    return lax.broadcasted_iota(jnp.int32, shape, dim)


def _rope_chunk(x, cos_t, sin_t, ll):
    fwd = pltpu.roll(x, LANES - ROT_DIM // 2, 1)
    bwd = pltpu.roll(x, ROT_DIM // 2, 1)
    partner = jnp.where(ll < ROT_DIM // 2, fwd, bwd)
    return x * cos_t + partner * sin_t


def _inproj_kernel(x_ref, nw_ref, w_ref, invf_ref, *rest, tm, seq_len, pos0, dims_major, n_carried):
    (z_ref, xbc_ref, sc_ref, qraw_ref, qrope_ref,
     kcmp_ref, vcmp_ref, ksel_ref, vsel_ref, kwin_ref, vwin_ref, small_ref,
     kcmp_rows_ref, vcmp_rows_ref, kselb_ref, vselb_ref, kwinb_ref, vwinb_ref, y_scr) = rest[n_carried:]
    i = pl.program_id(0)
    y_scr[...] = _dot(_rms(x_ref[...], nw_ref[...]).astype(BF16), w_ref[...])

    def proj(c0, width):
        return y_scr[:, c0:c0 + width]

    z_ref[...] = proj(COL_Z, SSD_WIDTH)
    xbc_ref[...] = proj(COL_XBC, SSD_XBC)
    sc_ref[...] = proj(COL_SC, 3 * SC_WIDTH)

    row = i * tm + _iota((tm, LANES), 0)
    pos = (pos0 + (row & (seq_len - 1))).astype(F32)
    lane = _iota((tm, LANES), 1)
    ll = lane & (HEAD_DIM - 1)
    ang = pos * invf_ref[...]
    cos_a = jnp.cos(ang)
    sin_a = jnp.sin(ang)
    half = ROT_DIM // 2
    cos_t = jnp.where(ll < ROT_DIM, cos_a, 1.0)
    sin_t = jnp.where(ll < half, -sin_a, jnp.where(ll < ROT_DIM, sin_a, 0.0))

    q = proj(COL_Q, ATT_WIDTH)
    qraw_ref[...] = q.astype(BF16)
    qrope_ref[...] = jnp.concatenate(
        [_rope_chunk(q[:, c * LANES:(c + 1) * LANES], cos_t, sin_t, ll) for c in range(ATT_WIDTH // LANES)],
        axis=1).astype(BF16)

    kv = proj(COL_KV, 6 * KV_WIDTH)
    kcmp = kv[:, 0:LANES]
    vcmp = kv[:, LANES:2 * LANES]
    ksel = _rope_chunk(kv[:, 2 * LANES:3 * LANES], cos_t, sin_t, ll)
    vsel = kv[:, 3 * LANES:4 * LANES]
    kwin = _rope_chunk(kv[:, 4 * LANES:5 * LANES], cos_t, sin_t, ll)
    vwin = kv[:, 5 * LANES:6 * LANES]
    for ref, val in ((kcmp_ref, kcmp), (vcmp_ref, vcmp), (ksel_ref, ksel), (vsel_ref, vsel),
                     (kwin_ref, kwin), (vwin_ref, vwin)):
        if dims_major:
            ref[0, 0] = val.T
        else:
            ref[...] = val
    kcmp_rows_ref[...] = kcmp
    vcmp_rows_ref[...] = vcmp
    kselb_ref[...] = ksel.astype(BF16)
    vselb_ref[...] = vsel.astype(BF16)
    kwinb_ref[...] = kwin.astype(BF16)
    vwinb_ref[...] = vwin.astype(BF16)

    sm = proj(COL_SMALL, LANES)
    is_gate = (lane >= GATE_LANE0) & (lane < GATE_LANE0 + 3 * ATT_HEADS)
    small_ref[...] = jnp.where(is_gate, _sigmoid(sm), sm)


def _inproj(h, norm_w, w_perm, invf, *, batch, seq_len, pos0, layer, depth, carried):
    t = h.shape[0]
    tm = min(INPROJ_ROWS, t)
    assert t % tm == 0 and seq_len & (seq_len - 1) == 0
    dims_major = seq_len % tm == 0
    tiles_per_row = max(seq_len // tm, 1)
    row_spec = lambda w: pl.BlockSpec((tm, w), lambda i: (i, 0))
    const_spec = lambda a: pl.BlockSpec(a.shape, lambda i: (0, 0))
    row_out = lambda w, dt: (row_spec(w), jax.ShapeDtypeStruct((t, w), dt))
    if dims_major:
        kv_out = (pl.BlockSpec((1, 1, KV_WIDTH, tm), lambda i: (layer, i // tiles_per_row, 0, i % tiles_per_row)),
                  jax.ShapeDtypeStruct((depth, batch, KV_WIDTH, seq_len), F32))
    else:
        kv_out = row_out(KV_WIDTH, F32)
    outs = ([row_out(w, F32) for w in (SSD_WIDTH, SSD_XBC, 3 * SC_WIDTH)] + [row_out(ATT_WIDTH, BF16)] * 2
            + [kv_out] * 6 + [row_out(LANES, F32)] + [row_out(KV_WIDTH, F32)] * 2 + [row_out(KV_WIDTH, BF16)] * 4)
    first_kv_out = 5
    n_fixed_in = 4
    assert len(carried) in (0, 6) and (dims_major or not carried)
    return pl.pallas_call(
        functools.partial(_inproj_kernel, tm=tm, seq_len=seq_len, pos0=pos0, dims_major=dims_major,
                          n_carried=len(carried)),
        grid=(t // tm,),
        in_specs=([row_spec(D_MODEL), const_spec(norm_w), const_spec(w_perm), const_spec(invf)]
                  + [pl.BlockSpec(memory_space=pl.ANY)] * len(carried)),
        out_specs=[o[0] for o in outs],
        out_shape=[o[1] for o in outs],
        input_output_aliases={n_fixed_in + n: first_kv_out + n for n in range(len(carried))},
        scratch_shapes=[pltpu.VMEM((tm, IN_PAD), F32)],
        compiler_params=pltpu.CompilerParams(dimension_semantics=("arbitrary",), vmem_limit_bytes=VMEM_LIMIT_BYTES),
        name="inproj",
    )(h, norm_w, w_perm, invf, *carried)


def _pad_rows(x, rows):
    if x.shape[0] == rows:
        return x
    return jnp.concatenate([x, jnp.zeros((rows - x.shape[0], x.shape[1]), x.dtype)], axis=0)


def _mixer_kernel(z_ref, xbc_ref, sc_ref, small_ref, ssm0_ref, cssd0_ref, csc0_ref,
                  cw_ref, cb_ref, dtb_ref, alog_ref, dskip_ref, nw_ref, scw_ref, tril_ref, expand_ref,
                  yssd_ref, ysc_ref, ssm_out_ref, cssd_out_ref, csc_out_ref,
                  s_scr, xp_scr, scp_scr, *, cs, n_chunks):
    C = SSD_CHUNK
    c = pl.program_id(1)

    @pl.when(c == 0)
    def _():
        s_scr[...] = ssm0_ref[0]
        xp_scr[0:SUBLANES, :] = cssd0_ref[0]
        scp_scr[0:SUBLANES, :] = csc0_ref[0]

    xp_scr[SUBLANES:SUBLANES + C, :] = _pad_rows(xbc_ref[...], C)
    cw = cw_ref[...]
    conv = cb_ref[...]
    for k in range(SSD_CONV):
        off = SUBLANES - (SSD_CONV - 1) + k
        conv = conv + cw[k:k + 1, :] * xp_scr[off:off + C, :]
    xbc_c = _silu(conv)
    tail_ssd = xp_scr[cs:cs + SUBLANES, :]
    xp_scr[0:SUBLANES, :] = tail_ssd

    xs = xbc_c[:, 0:SSD_WIDTH]
    bm = xbc_c[:, SSD_WIDTH:SSD_WIDTH + SSD_GROUPS * SSD_STATE]
    cm = xbc_c[:, SSD_WIDTH + SSD_GROUPS * SSD_STATE:SSD_XBC]

    lane = _iota((C, LANES), 1)
    rowi = _iota((C, LANES), 0)
    small = _pad_rows(small_ref[...], C)
    dt_raw = small + dtb_ref[...]
    dt = jnp.maximum(dt_raw, 0.0) + jnp.log1p(jnp.exp(-jnp.abs(dt_raw)))
    dt = jnp.where((lane < SSD_HEADS) & (rowi < cs), dt, 0.0)
    a_row = jnp.where(lane[0:1, :] < SSD_HEADS, -jnp.exp(alog_ref[...]), 0.0)
    acum = _dot_exact(tril_ref[...], dt * a_row)
    acum_t = acum.T
    a_last = acum[C - 1:C, :]
    expand = expand_ref[...]
    dt_w = _dot_exact(dt, expand)
    eacc_w = _dot_exact(jnp.exp(acum), expand)
    dend_w = _dot_exact(jnp.exp(a_last - acum), expand)
    cdec = jnp.exp(a_last)
    xdt = xs * dt_w
    xdtw_t = (xdt * dend_w).T

    causal = _iota((C, C), 0) >= _iota((C, C), 1)
    head_lo = lane < HEAD_DIM
    zeros_half = jnp.zeros((HEAD_DIM, SSD_STATE), BF16)
    y_chunks = [None] * (SSD_WIDTH // LANES)
    for g in range(SSD_GROUPS):
        bm_g = bm[:, g * SSD_STATE:(g + 1) * SSD_STATE].astype(BF16)
        cm_g = cm[:, g * SSD_STATE:(g + 1) * SSD_STATE].astype(BF16)
        cb = _dot_nt(cm_g, bm_g)
        for r in range(SSD_HEADS // SSD_GROUPS):
            h = g * (SSD_HEADS // SSD_GROUPS) + r
            ch, lo = h // 2, h % 2 == 0
            keep = head_lo if lo else jnp.logical_not(head_lo)
            seg = acum[:, h:h + 1] - acum_t[h:h + 1, :]
            decay = jnp.where(causal, jnp.exp(jnp.where(causal, seg, 0.0)), 0.0)
            xdt_h = jnp.where(keep, xdt[:, ch * LANES:(ch + 1) * LANES], 0.0).astype(BF16)
            y_d = _dot((cb * decay).astype(BF16), xdt_h)
            s_h = s_scr[h]
            s_b = s_h.astype(BF16)
            s_pl = jnp.concatenate([s_b, zeros_half] if lo else [zeros_half, s_b], axis=0)
            y_o = _dot_nt(cm_g, s_pl)
            y_h = y_d + eacc_w[:, ch * LANES:(ch + 1) * LANES] * y_o
            y_chunks[ch] = y_h if y_chunks[ch] is None else y_chunks[ch] + y_h
            s_scr[h] = s_h * cdec[:, h:h + 1] + _dot(xdtw_t[h * HEAD_DIM:(h + 1) * HEAD_DIM, :].astype(BF16), bm_g)
    y = jnp.concatenate(y_chunks, axis=1) + dskip_ref[...] * xs
    gated = y * _silu(_pad_rows(z_ref[...], C))
    yssd_ref[...] = _rms(gated, nw_ref[...])[0:cs]

    sc = _pad_rows(sc_ref[...], C)
    scp_scr[SUBLANES:SUBLANES + C, :] = sc[:, SC_WIDTH:2 * SC_WIDTH] * sc[:, 2 * SC_WIDTH:3 * SC_WIDTH]
    scw = scw_ref[...]
    conv3 = jnp.zeros((C, SC_WIDTH), F32)
    for k in range(SC_CONV):
        off = SUBLANES - (SC_CONV - 1) + k
        conv3 = conv3 + scw[k:k + 1, :] * scp_scr[off:off + C, :]
    ysc_ref[...] = (sc[:, 0:SC_WIDTH] * conv3)[0:cs]
    tail_sc = scp_scr[cs:cs + SUBLANES, :]
    scp_scr[0:SUBLANES, :] = tail_sc

    @pl.when(c == n_chunks - 1)
    def _():
        ssm_out_ref[0] = s_scr[...]
        cssd_out_ref[0] = tail_ssd
        csc_out_ref[0] = tail_sc


def _mixer(z, xbc, sc, small, ssm0, cssd0, csc0, lw, *, batch, seq_len):
    cs = min(SSD_CHUNK, seq_len)
    assert seq_len % cs == 0 and cs % SUBLANES == 0
    nc = seq_len // cs
    t = batch * seq_len
    row_spec = lambda w: pl.BlockSpec((cs, w), lambda b, c: (b * nc + c, 0))
    const_spec = lambda a: pl.BlockSpec(a.shape, lambda b, c: (0,) * a.ndim)
    batch_spec = lambda a: pl.BlockSpec((1,) + a.shape[1:], lambda b, c: (b,) + (0,) * (a.ndim - 1))
    consts = [lw["ssd_conv_w"], lw["ssd_conv_b"], lw["dt_bias"], lw["a_log"], lw["d_skip"], lw["ssd_norm_w"],
              lw["sc_conv_w"], lw["tril"], lw["expand"]]
    out_shape = [jax.ShapeDtypeStruct((t, SSD_WIDTH), F32), jax.ShapeDtypeStruct((t, SC_WIDTH), F32),
                 jax.ShapeDtypeStruct(ssm0.shape, F32), jax.ShapeDtypeStruct(cssd0.shape, F32),
                 jax.ShapeDtypeStruct(csc0.shape, F32)]
    return pl.pallas_call(
        functools.partial(_mixer_kernel, cs=cs, n_chunks=nc),
        grid=(batch, nc),
        in_specs=[row_spec(SSD_WIDTH), row_spec(SSD_XBC), row_spec(3 * SC_WIDTH), row_spec(LANES),
                  batch_spec(ssm0), batch_spec(cssd0), batch_spec(csc0)] + [const_spec(a) for a in consts],
        out_specs=[row_spec(SSD_WIDTH), row_spec(SC_WIDTH), batch_spec(ssm0), batch_spec(cssd0), batch_spec(csc0)],
        out_shape=out_shape,
        scratch_shapes=[pltpu.VMEM((SSD_HEADS, HEAD_DIM, SSD_STATE), F32),
                        pltpu.VMEM((SUBLANES + SSD_CHUNK, SSD_XBC), F32),
                        pltpu.VMEM((SUBLANES + SSD_CHUNK, SC_WIDTH), F32)],
        compiler_params=pltpu.CompilerParams(dimension_semantics=("arbitrary", "arbitrary"),
                                             vmem_limit_bytes=VMEM_LIMIT_BYTES),
        name="mixer",
    )(z, xbc, sc, small, ssm0, cssd0, csc0, *consts)


def _compress_hidden(read_rows, n_units, pe, wr):
    units = jnp.concatenate([read_rows(r).astype(BF16) for r in range(CMP_STRIDE)], axis=1)
    acc = _dot(units, wr)
    pe_rows = jnp.concatenate([jnp.broadcast_to(pe[h:h + 1, :], (SUBLANES // 2, pe.shape[1])) for h in range(2)], axis=0)
    pe_both = _dot(pe_rows.astype(BF16), wr)
    pe_term = pe_both[0:1, 0:KV_WIDTH] + pe_both[SUBLANES // 2:SUBLANES // 2 + 1, KV_WIDTH:2 * KV_WIDTH]
    return _silu(acc[:, 0:KV_WIDTH] + pltpu.roll(acc[:, KV_WIDTH:2 * KV_WIDTH], n_units - 1, 0) + pe_term)


def _compress_prompt_kernel(kc_ref, vc_ref, pe_ref, wr_ref, w2_ref, w2t_ref, ck_ref, cvt_ref, *, n_units):
    hid_k = _compress_hidden(lambda r: kc_ref[pl.ds(r, n_units, stride=CMP_STRIDE), :], n_units, pe_ref[0], wr_ref[0])
    ck_ref[0] = _dot(hid_k.astype(BF16), w2_ref[0]).astype(BF16)
    hid_v = _compress_hidden(lambda r: vc_ref[pl.ds(r, n_units, stride=CMP_STRIDE), :], n_units, pe_ref[1], wr_ref[1])
    cvt_ref[0] = _dot(w2t_ref[1], hid_v.T.astype(BF16)).astype(BF16)


def _compress_prompt(kc, vc, lw, *, batch, seq_len):
    n_units = seq_len // CMP_STRIDE
    consts = [lw["cmp_pe"], lw["cmp_wr"], lw["cmp_w2"], lw["cmp_w2t"]]
    const_spec = lambda a: pl.BlockSpec(a.shape, lambda b: (0,) * a.ndim)
    in_spec = pl.BlockSpec((seq_len, KV_WIDTH), lambda b: (b, 0))
    return pl.pallas_call(
        functools.partial(_compress_prompt_kernel, n_units=n_units),
        grid=(batch,),
        in_specs=[in_spec, in_spec] + [const_spec(a) for a in consts],
        out_specs=[pl.BlockSpec((1, n_units, KV_WIDTH), lambda b: (b, 0, 0)),
                   pl.BlockSpec((1, KV_WIDTH, n_units), lambda b: (b, 0, 0))],
        out_shape=[jax.ShapeDtypeStruct((batch, n_units, KV_WIDTH), BF16),
                   jax.ShapeDtypeStruct((batch, KV_WIDTH, n_units), BF16)],
        compiler_params=pltpu.CompilerParams(dimension_semantics=("arbitrary",), vmem_limit_bytes=VMEM_LIMIT_BYTES),
        name="compress_prompt",
    )(kc, vc, *consts)


def _page_copy(cache_hbm, layer, page, buf, slot, p, sem):
    return pltpu.make_async_copy(cache_hbm.at[layer, page], buf.at[slot, p], sem)


def _pages_dims_major(buf, slot, first_page, n):
    return jnp.concatenate([buf[slot, first_page + p] for p in range(n)], axis=1)


def _gather_pages(pt_ref, b, cache_hbm, layer, buf, slot, sem, n_pages):
    def start(p, carry):
        _page_copy(cache_hbm, layer, pt_ref[b, p], buf, slot, p, sem).start()
        return carry
    lax.fori_loop(0, n_pages, start, 0)


def _wait_pages(cache_hbm, layer, buf, slot, sem, n_pages):
    def wait(p, carry):
        _page_copy(cache_hbm, layer, 0, buf, slot, p, sem).wait()
        return carry
    lax.fori_loop(0, n_pages, wait, 0)


def _prefetch_pages(pt_ref, cache_hbm, layer, buf, sems, n_pages):
    b = pl.program_id(0)
    slot = b % 2

    @pl.when(b == 0)
    def _():
        _gather_pages(pt_ref, 0, cache_hbm, layer, buf, 0, sems.at[0], n_pages)

    @pl.when(b + 1 < pl.num_programs(0))
    def _():
        _gather_pages(pt_ref, b + 1, cache_hbm, layer, buf, 1 - slot, sems.at[1 - slot], n_pages)
    return slot


def _compress_paged_kernel(pt_ref, kc_hbm, vc_hbm, pe_ref, wr_ref, w2_ref, ck_ref, cv_ref,
                           kbuf, vbuf, rows_scr, ksems, vsems, *, layer, n_pages):
    n_units = n_pages * (PAGE_SIZE // CMP_STRIDE)
    slot = _prefetch_pages(pt_ref, kc_hbm, layer, kbuf, ksems, n_pages)
    _prefetch_pages(pt_ref, vc_hbm, layer, vbuf, vsems, n_pages)
    for idx, (src_hbm, buf, sems, dst) in enumerate(((kc_hbm, kbuf, ksems, ck_ref), (vc_hbm, vbuf, vsems, cv_ref))):
        _wait_pages(src_hbm, layer, buf, slot, sems.at[slot], n_pages)

        def to_rows(c, carry, buf=buf):
            for p in range(pages_per_trip):
                page = c * pages_per_trip + p
                rows_scr[pl.ds(pl.multiple_of(page * PAGE_SIZE, PAGE_SIZE), PAGE_SIZE), :] = buf[slot, page].T
            return carry
        pages_per_trip = math.gcd(n_pages, 8)
        lax.fori_loop(0, n_pages // pages_per_trip, to_rows, 0)

        hid = _compress_hidden(lambda r: rows_scr[pl.ds(r, n_units, stride=CMP_STRIDE), :], n_units,
                               pe_ref[idx], wr_ref[idx])
        dst[0] = _dot(hid.astype(BF16), w2_ref[idx]).astype(BF16)


def _compress_paged(page_table, cache_kt, cache_vt, lw, *, layer):
    batch, n_pages = page_table.shape
    past = n_pages * PAGE_SIZE
    n_units = past // CMP_STRIDE
    consts = [lw["cmp_pe"], lw["cmp_wr"], lw["cmp_w2"]]
    const_spec = lambda a: pl.BlockSpec(a.shape, lambda b, pt: (0,) * a.ndim)
    any_spec = pl.BlockSpec(memory_space=pl.ANY)
    out_spec = pl.BlockSpec((1, n_units, KV_WIDTH), lambda b, pt: (b, 0, 0))
    grid_spec = pltpu.PrefetchScalarGridSpec(
        num_scalar_prefetch=1, grid=(batch,),
        in_specs=[any_spec, any_spec] + [const_spec(a) for a in consts],
        out_specs=[out_spec, out_spec],
        scratch_shapes=[pltpu.VMEM((2, n_pages, KV_WIDTH, PAGE_SIZE), F32), pltpu.VMEM((2, n_pages, KV_WIDTH, PAGE_SIZE), F32),
                        pltpu.VMEM((past, KV_WIDTH), F32),
                        pltpu.SemaphoreType.DMA((2,)), pltpu.SemaphoreType.DMA((2,))])
    return pl.pallas_call(
        functools.partial(_compress_paged_kernel, layer=layer, n_pages=n_pages),
        grid_spec=grid_spec,
        out_shape=[jax.ShapeDtypeStruct((batch, n_units, KV_WIDTH), BF16)] * 2,
        compiler_params=pltpu.CompilerParams(dimension_semantics=("arbitrary",), vmem_limit_bytes=VMEM_LIMIT_BYTES),
        name="compress_paged",
    )(page_table, cache_kt, cache_vt, *consts)


def _stack_heads(q):
    lane = _iota((q.shape[0], LANES), 1)
    lo = lane < HEAD_DIM
    c0, c1, c2 = (q[:, c * LANES:(c + 1) * LANES] for c in range(3))
    blocks = [jnp.where(lo, c0, 0.0), jnp.where(lo, pltpu.roll(c0, HEAD_DIM, 1), 0.0), jnp.where(lo, c1, 0.0),
              jnp.where(lo, 0.0, c1), jnp.where(lo, 0.0, pltpu.roll(c2, HEAD_DIM, 1)), jnp.where(lo, 0.0, c2)]
    return jnp.concatenate(blocks, axis=0)


def _unstack_heads(o, rows):
    lane = _iota((rows, LANES), 1)
    lo = lane < HEAD_DIM
    blk = [o[j * rows:(j + 1) * rows] for j in range(ATT_HEADS)]
    return jnp.concatenate([jnp.where(lo, blk[0], pltpu.roll(blk[1], HEAD_DIM, 1)),
                            jnp.where(lo, blk[2], blk[3]),
                            jnp.where(lo, pltpu.roll(blk[4], HEAD_DIM, 1), blk[5])], axis=1)


def _tile_heads(x, reps):
    return jnp.concatenate([x] * reps, axis=0)


def _masked_softmax(s, mask):
    s = jnp.where(mask, s, NEG)
    e = jnp.where(mask, jnp.exp(s - jnp.max(s, axis=1, keepdims=True)), 0.0)
    l = jnp.sum(e, axis=1, keepdims=True)
    return e * (1.0 / jnp.where(l > 0.0, l, 1.0))


def _split_bf16(x):
    hi = x.astype(BF16)
    return hi, (x - hi.astype(F32)).astype(BF16)


def _select_blocks(imp, tq, axis):
    n_blocks = imp.shape[axis]
    blk = _iota(imp.shape, axis)
    cur = tq >> 6
    causal = blk * SEL_BLOCK <= tq
    forced = causal & ((blk == 0) | (blk == cur) | (blk == cur - 1))
    score = jnp.where(forced, FORCE, jnp.where(causal, imp, -FORCE))
    blk_f = blk.astype(F32)
    sel = jnp.zeros(imp.shape, F32)
    for _ in range(N_SEL):
        mx = jnp.max(score, axis=axis, keepdims=True)
        first = jnp.min(jnp.where(score == mx, blk_f, float(n_blocks)), axis=axis, keepdims=True)
        pick = blk_f == first
        sel = jnp.where(pick & (mx > -FORCE / 2), 1.0, sel)
        score = jnp.where(pick, -jnp.inf, score)
    return sel


def _combine_gates(gates, o_c, o_s, o_w, rows):
    out = []
    for hh in range(ATT_HEADS):
        sl = slice(hh * rows, (hh + 1) * rows)
        g0 = GATE_LANE0 + 3 * hh
        out.append(gates[:, g0:g0 + 1] * o_c[sl] + gates[:, g0 + 1:g0 + 2] * o_s[sl] + gates[:, g0 + 2:g0 + 3] * o_w[sl])
    return jnp.concatenate(out, axis=0)


def _attn_prompt_kernel(qr_ref, qn_ref, small_ref, ck_ref, cvt_ref, ks_ref, vs_ref, kw_ref, vw_ref,
                        onehot_ref, mimp_ref, out_ref,
                        qaug_scr, sa_scr, sb_scr, p_scr, m_scr, l_scr, a_scr, acc_scr,
                        sct_scr, pct_scr, psum_scr, sw_scr, pw_scr, lw_scr, oc_scr, ow_scr):
    R = ATTN_ROWS
    TK = KEYS_PER_TILE
    i = pl.program_id(1)
    t0 = pl.multiple_of(i * R, R)
    scale = HEAD_DIM ** -0.5 * math.log2(math.e)
    tq = t0 + _iota((R, 1), 0)
    tq_row = t0 + _iota((1, R), 1)
    q_rope = _stack_heads(qr_ref[...].astype(F32) * scale).astype(BF16)
    q_raw = _stack_heads(qn_ref[...].astype(F32) * scale).astype(BF16)
    qaug_scr[:, 0:LANES] = q_rope

    span = WINDOW + R
    s0 = pl.multiple_of(jnp.maximum(t0 - WINDOW, 0), R)
    sw_scr[...] = _dot_nt(q_rope, kw_ref[pl.ds(s0, span), :])
    spos = s0 + _iota((1, span), 1)
    win_bias = jnp.where((spos <= tq) & (spos > tq - WINDOW), 0.0, NEG)
    for j in range(ATT_HEADS):
        rows = slice(j * R, (j + 1) * R)
        s = sw_scr[rows, :] + win_bias
        e = jnp.exp2(s - jnp.max(s, axis=1, keepdims=True))
        pw_scr[rows, :] = e.astype(BF16)
        lw_scr[rows, :] = jnp.broadcast_to(jnp.sum(e, axis=1, keepdims=True), (R, LANES))
    ow_scr[...] = _dot(pw_scr[...], vw_ref[pl.ds(s0, span), :]) * (1.0 / lw_scr[...])

    n_units = ck_ref.shape[1]
    sct_scr[...] = _dot_nt(ck_ref[0], q_raw)
    cmp_end = _iota((n_units, 1), 0) * CMP_STRIDE + (CMP_LEN - 1)
    cmp_bias = jnp.where(cmp_end <= tq_row, 0.0, NEG)
    for j in range(ATT_HEADS):
        cols = slice(j * R, (j + 1) * R)
        s = sct_scr[:, cols] + cmp_bias
        mx = jnp.max(s, axis=0, keepdims=True)
        e = jnp.exp2(s - mx)
        inv = jnp.where(mx > NEG / 2, 1.0 / jnp.sum(e, axis=0, keepdims=True), 0.0)
        p = e * inv
        pct_scr[:, cols] = p.astype(BF16)
        k, g = divmod(j, GQA)
        if g == 0:
            psum_scr[k] = p
        else:
            psum_scr[k] += p
    o_c_t = _dot(cvt_ref[0], pct_scr[...])
    oc_scr[...] = jnp.concatenate([o_c_t[:, j * R:(j + 1) * R].T for j in range(ATT_HEADS)], axis=0)

    for k in range(ATT_KV_HEADS):
        hi, lo = _split_bf16(psum_scr[k])
        imp_t = _dot(mimp_ref[...], hi) + _dot(mimp_ref[...], lo)
        sel_t = _select_blocks(imp_t, tq_row, 0)
        bias = jnp.where(sel_t.T > 0.5, 0.0, NEG).astype(BF16)
        for g in range(GQA):
            j = k * GQA + g
            qaug_scr[j * R:(j + 1) * R, LANES:2 * LANES] = bias

    def scores(k0, s_ref):
        k_aug = jnp.concatenate([ks_ref[pl.ds(k0, TK), :], onehot_ref[pl.ds(k0, TK), :]], axis=1)
        s_ref[...] = _dot_nt(qaug_scr[...], k_aug)

    def accumulate(k0, s_ref, first):
        if first:
            causal_bias = jnp.where((k0 + _iota((1, TK), 1)) <= tq, 0.0, NEG)
        for j in range(ATT_HEADS):
            rows = slice(j * R, (j + 1) * R)
            s = s_ref[rows, :]
            if first:
                s = s + causal_bias
                m_new = jnp.broadcast_to(jnp.max(s, axis=1, keepdims=True), (R, LANES))
            else:
                m_old = m_scr[rows, :]
                m_new = jnp.maximum(m_old, jnp.max(s, axis=1, keepdims=True))
                alpha = jnp.exp2(m_old - m_new)
            p = jnp.exp2(s - jnp.concatenate([m_new] * (TK // LANES), axis=1))
            l_new = jnp.sum(p, axis=1, keepdims=True)
            p_scr[rows, :] = p.astype(BF16)
            m_scr[rows, :] = m_new
            if first:
                l_scr[rows, :] = jnp.broadcast_to(l_new, (R, LANES))
            else:
                l_scr[rows, :] = alpha * l_scr[rows, :] + l_new
                a_scr[rows, :] = alpha
        pv = _dot(p_scr[...], vs_ref[pl.ds(k0, TK), :])
        if first:
            acc_scr[...] = pv
        else:
            acc_scr[...] = a_scr[...] * acc_scr[...] + pv

    n_past = t0 // TK
    k_diag = pl.multiple_of(n_past * TK, TK)
    scores(k_diag, sa_scr)
    accumulate(k_diag, sa_scr, True)

    @pl.when(n_past % 2 == 1)
    def _():
        k_odd = pl.multiple_of((n_past - 1) * TK, TK)
        scores(k_odd, sb_scr)
        accumulate(k_odd, sb_scr, False)

    n_pairs = n_past // 2

    @pl.when(n_pairs > 0)
    def _():
        scores(0, sa_scr)

        def pair_body(jj, carry):
            k_even = pl.multiple_of(2 * jj * TK, TK)
            k_odd = pl.multiple_of(k_even + TK, TK)
            scores(k_odd, sb_scr)
            accumulate(k_even, sa_scr, False)
            scores(pl.multiple_of(k_odd + TK, TK), sa_scr)
            accumulate(k_odd, sb_scr, False)
            return carry
        lax.fori_loop(0, n_pairs - 1, pair_body, 0)
        k_even = pl.multiple_of(2 * (n_pairs - 1) * TK, TK)
        k_odd = pl.multiple_of(k_even + TK, TK)
        scores(k_odd, sb_scr)
        accumulate(k_even, sa_scr, False)
        accumulate(k_odd, sb_scr, False)
    o_s = acc_scr[...] * (1.0 / l_scr[...])

    out_ref[...] = _unstack_heads(_combine_gates(small_ref[...], oc_scr[...], o_s, ow_scr[...], R), R)


def _attn_prompt(qrope, qraw, small, ck, cvt, kselb, vselb, kwinb, vwinb, *, batch, seq_len):
    span = WINDOW + ATTN_ROWS
    assert seq_len % KEYS_PER_TILE == 0 and seq_len >= span and seq_len // SEL_BLOCK <= LANES
    assert KEYS_PER_TILE % ATTN_ROWS == 0
    nqb = seq_len // ATTN_ROWS
    n_units = seq_len // CMP_STRIDE
    t = batch * seq_len
    pos = jnp.arange(seq_len, dtype=jnp.int32)
    onehot = (pos[:, None] // SEL_BLOCK == jnp.arange(LANES, dtype=jnp.int32)[None, :]).astype(BF16)
    m_imp_t = _importance_matrix(n_units, LANES).T
    q_spec = lambda w: pl.BlockSpec((ATTN_ROWS, w), lambda b, i: (b * nqb + i, 0))
    batch_spec = lambda a: pl.BlockSpec((1,) + a.shape[1:], lambda b, i: (b, 0, 0))
    seq_spec = pl.BlockSpec((seq_len, KV_WIDTH), lambda b, i: (b, 0))
    const_spec = lambda a: pl.BlockSpec(a.shape, lambda b, i: (0, 0))
    rows = ATT_HEADS * ATTN_ROWS
    vmem = pltpu.VMEM
    scratch = [vmem((rows, 2 * LANES), BF16),
               vmem((rows, KEYS_PER_TILE), F32), vmem((rows, KEYS_PER_TILE), F32),
               vmem((rows, KEYS_PER_TILE), BF16),
               vmem((rows, LANES), F32), vmem((rows, LANES), F32), vmem((rows, LANES), F32),
               vmem((rows, LANES), F32),
               vmem((n_units, rows), F32), vmem((n_units, rows), BF16),
               vmem((ATT_KV_HEADS, n_units, ATTN_ROWS), F32),
               vmem((rows, span), F32), vmem((rows, span), BF16), vmem((rows, LANES), F32),
               vmem((rows, LANES), F32), vmem((rows, LANES), F32)]
    return pl.pallas_call(
        _attn_prompt_kernel,
        grid=(batch, nqb),
        in_specs=[q_spec(ATT_WIDTH), q_spec(ATT_WIDTH), q_spec(LANES), batch_spec(ck), batch_spec(cvt),
                  seq_spec, seq_spec, seq_spec, seq_spec, const_spec(onehot), const_spec(m_imp_t)],
        out_specs=q_spec(ATT_WIDTH),
        out_shape=jax.ShapeDtypeStruct((t, ATT_WIDTH), F32),
        scratch_shapes=scratch,
        compiler_params=pltpu.CompilerParams(dimension_semantics=("arbitrary", "arbitrary"),
                                             vmem_limit_bytes=VMEM_LIMIT_BYTES),
        name="attn_prompt",
    )(qrope, qraw, small, ck, cvt, kselb, vselb, kwinb, vwinb, onehot, m_imp_t)


def _importance_matrix(n_units, width):
    n = jnp.arange(n_units, dtype=jnp.int32)[:, None]
    j = jnp.arange(width, dtype=jnp.int32)[None, :]
    upb = SEL_BLOCK // CMP_STRIDE
    m = (n // upb == j).astype(F32) + ((n + 1) // upb == j).astype(F32)
    return jnp.where(n < n_units - 1, m, 0.0).astype(BF16)


def _attn_sample_kernel(pt_ref, ks_hbm, vs_hbm, qr_ref, qn_ref, small_ref, ck_ref, cv_ref,
                        knew_ref, vnew_ref, kwc_ref, vwc_ref, kwnew_ref, vwnew_ref, onehot_ref, mimp_ref,
                        out_ref, kbuf, vbuf, ksems, vsems, *, layer, n_pages, past, win_buf, group_blocks):
    R = qr_ref.shape[0]
    slot = _prefetch_pages(pt_ref, ks_hbm, layer, kbuf, ksems, n_pages)
    _prefetch_pages(pt_ref, vs_hbm, layer, vbuf, vsems, n_pages)

    scale = HEAD_DIM ** -0.5
    tq = past + _iota((R, 1), 0)
    tq_all = _tile_heads(tq, ATT_HEADS)
    q_rope = _stack_heads(qr_ref[...].astype(F32) * scale).astype(BF16)
    q_raw = _stack_heads(qn_ref[...].astype(F32) * scale).astype(BF16)

    n_units = ck_ref.shape[1]
    s_c = _dot_nt(q_raw, ck_ref[0])
    cmp_end = _iota((1, n_units), 1) * CMP_STRIDE + (CMP_LEN - 1)
    p_c = _masked_softmax(s_c, cmp_end <= tq_all)
    o_c = _dot(p_c.astype(BF16), cv_ref[0])

    bias = []
    for k in range(ATT_KV_HEADS):
        hi, lo = _split_bf16(sum(p_c[(k * GQA + g) * R:(k * GQA + g + 1) * R] for g in range(GQA)))
        sel = _select_blocks(_dot(hi, mimp_ref[...]) + _dot(lo, mimp_ref[...]), tq, 1)
        bias.append(_tile_heads(jnp.where(sel > 0.5, 0.0, NEG), GQA))
    bias = jnp.concatenate(bias, axis=0).astype(BF16)

    k_off = past - win_buf
    s_wc = _dot(q_rope, kwc_ref[0, 0].astype(BF16))
    s_wn = _dot_nt(q_rope, kwnew_ref[...])
    pos_c = k_off + _iota((1, win_buf), 1)
    pos_n = past + _iota((1, R), 1)
    mask_c = (pos_c <= tq_all) & (pos_c > tq_all - WINDOW) & (pos_c >= k_off)
    mask_n = (pos_n <= tq_all) & (pos_n > tq_all - WINDOW) & (pos_n >= k_off)
    s_wc = jnp.where(mask_c, s_wc, NEG)
    s_wn = jnp.where(mask_n, s_wn, NEG)
    m_w = jnp.maximum(jnp.max(s_wc, axis=1, keepdims=True), jnp.max(s_wn, axis=1, keepdims=True))
    e_wc = jnp.where(mask_c, jnp.exp(s_wc - m_w), 0.0)
    e_wn = jnp.where(mask_n, jnp.exp(s_wn - m_w), 0.0)
    l_w = jnp.sum(e_wc, axis=1, keepdims=True) + jnp.sum(e_wn, axis=1, keepdims=True)
    o_w = _dot_nt(e_wc.astype(BF16), vwc_ref[0, 0].astype(BF16)) + _dot(e_wn.astype(BF16), vwnew_ref[...])
    o_w = o_w * (1.0 / jnp.where(l_w > 0.0, l_w, 1.0))

    s_n = _dot_nt(q_rope, knew_ref[...])
    mask_sn = pos_n <= tq_all
    s_n = jnp.where(mask_sn, s_n, NEG)
    _wait_pages(ks_hbm, layer, kbuf, slot, ksems.at[slot], n_pages)
    group_keys = group_blocks * SEL_BLOCK
    n_groups = (n_pages * PAGE_SIZE) // group_keys
    s_past = []
    group_pages = group_keys // PAGE_SIZE
    for gi in range(n_groups):
        q_aug = jnp.concatenate([q_rope, bias[:, gi * group_blocks:(gi + 1) * group_blocks]], axis=1)
        k_group = _pages_dims_major(kbuf, slot, gi * group_pages, group_pages)
        k_aug_t = jnp.concatenate([k_group.astype(BF16), onehot_ref[...]], axis=0)
        s_past.append(_dot(q_aug, k_aug_t))
    m_s = jnp.max(s_n, axis=1, keepdims=True)
    for s in s_past:
        m_s = jnp.maximum(m_s, jnp.max(s, axis=1, keepdims=True))
    e_n = jnp.where(mask_sn, jnp.exp(s_n - m_s), 0.0)
    l_s = jnp.sum(e_n, axis=1, keepdims=True)
    o_s = _dot(e_n.astype(BF16), vnew_ref[...])
    _wait_pages(vs_hbm, layer, vbuf, slot, vsems.at[slot], n_pages)
    for gi, s in enumerate(s_past):
        e = jnp.exp(s - m_s)
        l_s = l_s + jnp.sum(e, axis=1, keepdims=True)
        v_group = _pages_dims_major(vbuf, slot, gi * group_pages, group_pages)
        o_s = o_s + _dot_nt(e.astype(BF16), v_group.astype(BF16))
    o_s = o_s * (1.0 / l_s)

    out_ref[...] = _unstack_heads(_combine_gates(small_ref[...], o_c, o_s, o_w, R), R)


def _attn_sample(page_table, cache_kst, cache_vst, cache_kwt, cache_vwt, qrope, qraw, small, ck, cv,
                 knew, vnew, kwnew, vwnew, *, layer, seq_len):
    batch, n_pages = page_table.shape
    past = n_pages * PAGE_SIZE
    win_buf = cache_kwt.shape[3]
    assert seq_len == SUBLANES and (past + seq_len) // CMP_STRIDE == past // CMP_STRIDE and past >= WINDOW
    n_units = past // CMP_STRIDE
    nb_past = past // SEL_BLOCK
    group_blocks = min(LANES, nb_past)
    assert nb_past % group_blocks == 0
    n_blk_lanes = -(-(nb_past + 1) // LANES) * LANES
    key = jnp.arange(group_blocks * SEL_BLOCK, dtype=jnp.int32)
    onehot = (key[None, :] // SEL_BLOCK == jnp.arange(group_blocks, dtype=jnp.int32)[:, None]).astype(BF16)
    m_imp = _importance_matrix(n_units, n_blk_lanes)
    win_spec = pl.BlockSpec((1, 1, KV_WIDTH, win_buf), lambda b, pt: (layer, b, 0, 0))
    any_spec = pl.BlockSpec(memory_space=pl.ANY)
    row_spec = lambda w: pl.BlockSpec((seq_len, w), lambda b, pt: (b, 0))
    batch_spec = lambda a: pl.BlockSpec((1,) + a.shape[1:], lambda b, pt: (b, 0, 0))
    const_spec = lambda a: pl.BlockSpec(a.shape, lambda b, pt: (0, 0))
    grid_spec = pltpu.PrefetchScalarGridSpec(
        num_scalar_prefetch=1, grid=(batch,),
        in_specs=[any_spec, any_spec, row_spec(ATT_WIDTH), row_spec(ATT_WIDTH), row_spec(LANES),
                  batch_spec(ck), batch_spec(cv), row_spec(KV_WIDTH), row_spec(KV_WIDTH),
                  win_spec, win_spec, row_spec(KV_WIDTH), row_spec(KV_WIDTH),
                  const_spec(onehot), const_spec(m_imp)],
        out_specs=row_spec(ATT_WIDTH),
        scratch_shapes=[pltpu.VMEM((2, n_pages, KV_WIDTH, PAGE_SIZE), F32), pltpu.VMEM((2, n_pages, KV_WIDTH, PAGE_SIZE), F32),
                        pltpu.SemaphoreType.DMA((2,)), pltpu.SemaphoreType.DMA((2,))])
    return pl.pallas_call(
        functools.partial(_attn_sample_kernel, layer=layer, n_pages=n_pages, past=past, win_buf=win_buf,
                          group_blocks=group_blocks),
        grid_spec=grid_spec,
        out_shape=jax.ShapeDtypeStruct((batch * seq_len, ATT_WIDTH), F32),
        compiler_params=pltpu.CompilerParams(dimension_semantics=("arbitrary",), vmem_limit_bytes=VMEM_LIMIT_BYTES),
        name="attn_sample",
    )(page_table, cache_kst, cache_vst, qrope, qraw, small, ck, cv, knew, vnew, cache_kwt, cache_vwt, kwnew, vwnew,
      onehot, m_imp)


def _ffn_kernel(h_ref, yssd_ref, ysc_ref, yatt_ref, n2_ref, fn_ref, wo_hbm, w1_hbm, w2_hbm, out_ref,
                wo_scr, w1_scr, w2_scr, hid_scr, sems, *, final):
    @pl.when(pl.program_id(0) == 0)
    def _():
        copies = [pltpu.make_async_copy(src, dst, sems.at[n])
                  for n, (src, dst) in enumerate(((wo_hbm, wo_scr), (w1_hbm, w1_scr), (w2_hbm, w2_scr)))]
        for c in copies:
            c.start()
        for c in copies:
            c.wait()

    mix = (_dot(yssd_ref[...].astype(BF16), wo_scr[0:SSD_WIDTH, :])
           + _dot(ysc_ref[...].astype(BF16), wo_scr[SSD_WIDTH:SSD_WIDTH + SC_WIDTH, :])
           + _dot(yatt_ref[...].astype(BF16), wo_scr[SSD_WIDTH + SC_WIDTH:D_MODEL, :]))
    hn = h_ref[...] + mix
    xn = _rms(hn, n2_ref[...]).astype(BF16)
    hid_scr[...] = jnp.square(jnp.maximum(_dot(xn, w1_scr[...]), 0.0)).astype(BF16)
    o = hn + _dot(hid_scr[...], w2_scr[...])
    out_ref[...] = _rms(o, fn_ref[...]) if final else o


def _ffn(h, yssd, ysc, yatt, lw, final_norm_w, *, final):
    t = h.shape[0]
    tm = min(FFN_ROWS, t)
    assert t % tm == 0
    row_spec = lambda w: pl.BlockSpec((tm, w), lambda i: (i, 0))
    const_spec = lambda a: pl.BlockSpec(a.shape, lambda i: (0, 0))
    any_spec = pl.BlockSpec(memory_space=pl.ANY)
    return pl.pallas_call(
        functools.partial(_ffn_kernel, final=final),
        grid=(t // tm,),
        in_specs=[row_spec(D_MODEL), row_spec(SSD_WIDTH), row_spec(SC_WIDTH), row_spec(ATT_WIDTH),
                  const_spec(lw["norm2_w"]), const_spec(final_norm_w), any_spec, any_spec, any_spec],
        out_specs=row_spec(D_MODEL),
        out_shape=jax.ShapeDtypeStruct((t, D_MODEL), F32),
        scratch_shapes=[pltpu.VMEM((D_MODEL, D_MODEL), BF16), pltpu.VMEM((D_MODEL, D_FF), BF16),
                        pltpu.VMEM((D_FF, D_MODEL), BF16), pltpu.VMEM((tm, D_FF), BF16),
                        pltpu.SemaphoreType.DMA((3,))],
        compiler_params=pltpu.CompilerParams(dimension_semantics=("arbitrary",), vmem_limit_bytes=VMEM_LIMIT_BYTES),
        name="ffn",
    )(h, yssd, ysc, yatt, lw["norm2_w"], final_norm_w, lw["w_out"], lw["w_ff1"], lw["w_ff2"])


def _lane_row(v, width):
    v = v.reshape(1, -1).astype(F32)
    return jnp.pad(v, ((0, 0), (0, width - v.shape[1])))


def _prep_layer(l, norm1_w, w_in, ssd_conv_w, ssd_conv_b, ssd_dt_bias, ssd_a_log, ssd_d, ssd_norm_w, sc_conv_w,
                cmp_pe, cmp_w1, cmp_w2, w_out, norm2_w, w_ff1, w_ff2):
    offs = [0]
    for s in IN_SIZES:
        offs.append(offs[-1] + s)
    seg = lambda a: jnp.arange(offs[a], offs[a + 1])
    perm = jnp.concatenate([seg(0), seg(1), seg(3), seg(4), seg(5), seg(6)] + [seg(a) for a in range(7, 13)]
                           + [seg(2), seg(13)])
    w_perm = jnp.pad(w_in[l][:, perm], ((0, 0), (0, IN_PAD - offs[-1]))).astype(BF16)

    eye = jnp.eye(ATT_KV_HEADS, dtype=F32)
    w1 = cmp_w1[l].reshape(2, 2, CMP_STRIDE, HEAD_DIM, HEAD_DIM)
    wfs = jnp.einsum('vhrde,kK->vhrkdKe', w1, eye).reshape(2, 2, CMP_STRIDE * KV_WIDTH, KV_WIDTH).astype(BF16)
    pe = cmp_pe[l].reshape(2, 2, CMP_STRIDE, 1, HEAD_DIM)
    pe = jnp.broadcast_to(pe, (2, 2, CMP_STRIDE, ATT_KV_HEADS, HEAD_DIM)).reshape(2, 2, CMP_STRIDE * KV_WIDTH)
    w2 = jnp.einsum('vde,kK->vkdKe', cmp_w2[l], eye).reshape(2, KV_WIDTH, KV_WIDTH).astype(BF16)

    head = jnp.arange(LANES, dtype=jnp.int32)[:, None]
    col = jnp.arange(SSD_WIDTH, dtype=jnp.int32)[None, :]
    idx = jnp.arange(SSD_CHUNK, dtype=jnp.int32)
    return {
        "norm1_w": norm1_w[l].reshape(1, D_MODEL), "w_in": w_perm,
        "ssd_conv_w": ssd_conv_w[l], "ssd_conv_b": ssd_conv_b[l].reshape(1, SSD_XBC),
        "dt_bias": _lane_row(ssd_dt_bias[l], LANES), "a_log": _lane_row(ssd_a_log[l], LANES),
        "d_skip": jnp.repeat(ssd_d[l], HEAD_DIM).reshape(1, SSD_WIDTH),
        "ssd_norm_w": ssd_norm_w[l].reshape(1, SSD_WIDTH), "sc_conv_w": sc_conv_w[l],
        "tril": (idx[:, None] >= idx[None, :]).astype(F32),
        "expand": (col // HEAD_DIM == head).astype(F32),
        "cmp_pe": pe, "cmp_w2": w2,
        "cmp_wr": jnp.concatenate([wfs[:, 0], wfs[:, 1]], axis=-1), "cmp_w2t": w2.transpose(0, 2, 1),
        "w_out": w_out[l].astype(BF16), "norm2_w": norm2_w[l].reshape(1, D_MODEL),
        "w_ff1": w_ff1[l].astype(BF16), "w_ff2": w_ff2[l].astype(BF16),
    }


def _front_pad_rows(a, rows):
    return jnp.pad(a, ((0, 0), (rows - a.shape[1], 0), (0, 0)))


def _run_layer(h, lw, invf, ssm0, cssd0, csc0, attn_fn, final_norm_w, *, batch, seq_len, pos0, final,
               layer, depth, carried):
    (z, xbc, sc, qraw, qrope, kcmp, vcmp, ksel, vsel, kwin, vwin, small, kcmp_rows, vcmp_rows,
     kselb, vselb, kwinb, vwinb) = _inproj(h, lw["norm1_w"], lw["w_in"], invf, batch=batch, seq_len=seq_len, pos0=pos0,
                                           layer=layer, depth=depth, carried=carried)
    yssd, ysc, ssm_new, cssd_new, csc_new = _mixer(
        z, xbc, sc, small, ssm0, _front_pad_rows(cssd0, SUBLANES), _front_pad_rows(csc0, SUBLANES), lw,
        batch=batch, seq_len=seq_len)
    rows = dict(kcmp=kcmp, vcmp=vcmp, ksel=ksel, vsel=vsel, kwin=kwin, vwin=vwin, kcmp_rows=kcmp_rows,
                vcmp_rows=vcmp_rows, kselb=kselb, vselb=vselb, kwinb=kwinb, vwinb=vwinb)
    yatt = attn_fn(qrope, qraw, small, rows)
    h = _ffn(h, yssd, ysc, yatt, lw, final_norm_w, final=final)
    states = (ssm_new, cssd_new[:, SUBLANES - (SSD_CONV - 1):], csc_new[:, SUBLANES - (SC_CONV - 1):])
    return h, states, rows


def kernel(x_prompt, x_sample, cache_k_cmp, cache_v_cmp, cache_k_sel, cache_v_sel, cache_k_win, cache_v_win, state_ssm, state_ssd_conv, state_sc_conv, page_table, norm1_w, w_in, ssd_conv_w, ssd_conv_b, ssd_dt_bias, ssd_a_log, ssd_d, ssd_norm_w, sc_conv_w, cmp_pe, cmp_w1, cmp_w2, w_out, norm2_w, w_ff1, w_ff2, final_norm_w):
    bp, lp, _ = x_prompt.shape
    db, ls, _ = x_sample.shape
    depth = w_in.shape[0]
    n_pages = page_table.shape[1]
    past = n_pages * PAGE_SIZE
    win_buf = cache_k_win.shape[2]
    fnw = final_norm_w.reshape(1, D_MODEL)

    half = ROT_DIM // 2
    inv_freq = ROPE_THETA ** (-jnp.arange(half, dtype=F32) * 2.0 / ROT_DIM)
    invf = jnp.tile(inv_freq, LANES // half).reshape(1, LANES)

    hp = x_prompt.reshape(bp * lp, D_MODEL)
    hs = x_sample.reshape(db * ls, D_MODEL)
    ssm0_p = jnp.zeros((bp, SSD_HEADS, HEAD_DIM, SSD_STATE), F32)
    cssd0_p = jnp.zeros((bp, SSD_CONV - 1, SSD_XBC), F32)
    csc0_p = jnp.zeros((bp, SC_CONV - 1, SC_WIDTH), F32)
    kv_names = ("kcmp", "vcmp", "ksel", "vsel", "kwin", "vwin")
    p_kv_layers = []
    p_state_lists = [[] for _ in range(3)]
    s_lists = [[] for _ in range(9)]
    keep_p = min(WINDOW, lp)
    keep_s = min(WINDOW, win_buf + ls)
    prompt_dims_major = lp % min(INPROJ_ROWS, bp * lp) == 0
    carried_p = [jnp.zeros((depth, bp, KV_WIDTH, lp), F32) for _ in kv_names] if prompt_dims_major else []

    def kv4(a, b, n):
        return a.reshape(b, n, ATT_KV_HEADS, HEAD_DIM)

    def dims_major(c):
        return c.transpose(0, 1, 3, 4, 2).reshape(c.shape[0], c.shape[1], KV_WIDTH, c.shape[2])

    pages_kc, pages_vc, pages_ks, pages_vs = (dims_major(c) for c in (cache_k_cmp, cache_v_cmp, cache_k_sel, cache_v_sel))
    win_k, win_v = dims_major(cache_k_win), dims_major(cache_v_win)

    for l in range(depth):
        lw = _prep_layer(l, norm1_w, w_in, ssd_conv_w, ssd_conv_b, ssd_dt_bias, ssd_a_log, ssd_d, ssd_norm_w,
                         sc_conv_w, cmp_pe, cmp_w1, cmp_w2, w_out, norm2_w, w_ff1, w_ff2)
        final = l == depth - 1

        def prompt_attn(qrope, qraw, small, rows, lw=lw):
            ck, cvt = _compress_prompt(rows["kcmp_rows"], rows["vcmp_rows"], lw, batch=bp, seq_len=lp)
            return _attn_prompt(qrope, qraw, small, ck, cvt, rows["kselb"], rows["vselb"], rows["kwinb"], rows["vwinb"],
                                batch=bp, seq_len=lp)

        def sample_attn(qrope, qraw, small, rows, lw=lw, l=l):
            ck, cv = _compress_paged(page_table, pages_kc, pages_vc, lw, layer=l)
            return _attn_sample(page_table, pages_ks, pages_vs, win_k, win_v, qrope, qraw, small, ck, cv,
                                rows["kselb"], rows["vselb"], rows["kwinb"], rows["vwinb"], layer=l, seq_len=ls)

        hp, st_p, rows_p = _run_layer(hp, lw, invf, ssm0_p, cssd0_p, csc0_p, prompt_attn, fnw,
                                      batch=bp, seq_len=lp, pos0=0, final=final, layer=l, depth=depth, carried=carried_p)
        hs, st_s, rows_s = _run_layer(hs, lw, invf, state_ssm[l], state_ssd_conv[l], state_sc_conv[l], sample_attn, fnw,
                                      batch=db, seq_len=ls, pos0=past, final=final, layer=l, depth=depth, carried=[])
        if prompt_dims_major:
            carried_p = [rows_p[n] for n in kv_names]
        else:
            p_kv_layers.append([kv4(rows_p[n], bp, lp) for n in kv_names])
        for lst, arr in zip(p_state_lists, st_p):
            lst.append(arr)
        s_new = [kv4(rows_s[n], db, ls) for n in ("kcmp", "vcmp", "ksel", "vsel")]
        s_new += [jnp.concatenate([cache_k_win[l], kv4(rows_s["kwin"], db, ls)], axis=1)[:, -keep_s:],
                  jnp.concatenate([cache_v_win[l], kv4(rows_s["vwin"], db, ls)], axis=1)[:, -keep_s:]]
        s_new += list(st_s)
        for lst, arr in zip(s_lists, s_new):
            lst.append(arr)

    if prompt_dims_major:
        p_kv = [a.reshape(depth, bp, ATT_KV_HEADS, HEAD_DIM, lp).transpose(0, 1, 4, 2, 3) for a in carried_p]
    else:
        p_kv = [jnp.stack([lay[n] for lay in p_kv_layers]) for n in range(len(kv_names))]
    p_kv[4], p_kv[5] = p_kv[4][:, :, -keep_p:], p_kv[5][:, :, -keep_p:]
    p_out = p_kv + [jnp.stack(a) for a in p_state_lists]
    s_out = [jnp.stack(a) for a in s_lists]
    y_prompt = hp.reshape(bp, lp, D_MODEL)
    y_sample = hs.reshape(db, ls, D_MODEL)
    return (y_prompt, y_sample, *p_out, *s_out)
```

```python
import functools
import math

import jax
import jax.numpy as jnp
from jax import lax
from jax.experimental import pallas as pl
from jax.experimental.pallas import tpu as pltpu

F32 = jnp.float32
BF16 = jnp.bfloat16
HIGHEST = lax.Precision.HIGHEST

D_MODEL = 1024
HEAD_DIM = 64
SSD_WIDTH = 384
SSD_HEADS = 6
SSD_GROUPS = 2
SSD_STATE = 128
SSD_CONV = 4
SSD_CHUNK = 128
SSD_XBC = SSD_WIDTH + 2 * SSD_GROUPS * SSD_STATE
SC_WIDTH = 256
SC_CONV = 3
ATT_WIDTH = 384
ATT_HEADS = 6
ATT_KV_HEADS = 2
GQA = ATT_HEADS // ATT_KV_HEADS
KV_WIDTH = ATT_KV_HEADS * HEAD_DIM
CMP_LEN = 32
CMP_STRIDE = 16
SEL_BLOCK = 64
N_SEL = 16
WINDOW = 512
PAGE_SIZE = 128
ROPE_THETA = 500000.0
ROT_DIM = HEAD_DIM // 4
D_FF = 4 * D_MODEL
EPS = 1e-6
NEG = -1e30
FORCE = 1e4
IN_SIZES = (SSD_WIDTH, SSD_XBC, SSD_HEADS, SC_WIDTH, SC_WIDTH, SC_WIDTH, ATT_WIDTH,
            KV_WIDTH, KV_WIDTH, KV_WIDTH, KV_WIDTH, KV_WIDTH, KV_WIDTH, ATT_HEADS * 3)

LANES = 128
SUBLANES = 8
VMEM_LIMIT_BYTES = 56 * 1024 * 1024

COL_Z = 0
COL_XBC = COL_Z + SSD_WIDTH
COL_SC = COL_XBC + SSD_XBC
COL_Q = COL_SC + 3 * SC_WIDTH
COL_KV = COL_Q + ATT_WIDTH
COL_SMALL = COL_KV + 6 * KV_WIDTH
IN_PAD = COL_SMALL + LANES
GATE_LANE0 = SSD_HEADS
KEYS_PER_TILE = 512
ATTN_ROWS = 256
INPROJ_ROWS = 512
FFN_ROWS = 512


def _dot(a, b):
    return jnp.dot(a, b, preferred_element_type=F32)


def _dot_nt(a, b):
    return lax.dot_general(a, b, (((1,), (1,)), ((), ())), preferred_element_type=F32)


def _dot_exact(a, b):
    return jnp.dot(a, b, preferred_element_type=F32, precision=HIGHEST)


def _silu(x):
    return x * (1.0 / (1.0 + jnp.exp(-x)))


def _sigmoid(x):
    return 1.0 / (1.0 + jnp.exp(-x))


def _rms(x, w):
    return x * lax.rsqrt(jnp.mean(x * x, axis=-1, keepdims=True) + EPS) * w


def _iota(shape, dim):
    return lax.broadcasted_iota(jnp.int32, shape, dim)


def _rope_chunk(x, cos_t, sin_t, ll):
    fwd = pltpu.roll(x, LANES - ROT_DIM // 2, 1)
    bwd = pltpu.roll(x, ROT_DIM // 2, 1)
    partner = jnp.where(ll < ROT_DIM // 2, fwd, bwd)
    return x * cos_t + partner * sin_t


def _inproj_kernel(x_ref, nw_ref, w_ref, invf_ref, *rest, tm, seq_len, pos0, dims_major, n_carried):
    (z_ref, xbc_ref, sc_ref, qraw_ref, qrope_ref,
     kcmp_ref, vcmp_ref, ksel_ref, vsel_ref, kwin_ref, vwin_ref, small_ref,
     kcmp_rows_ref, vcmp_rows_ref, kselb_ref, vselb_ref, kwinb_ref, vwinb_ref, y_scr) = rest[n_carried:]
    i = pl.program_id(0)
    y_scr[...] = _dot(_rms(x_ref[...], nw_ref[...]).astype(BF16), w_ref[...])

    def proj(c0, width):
        return y_scr[:, c0:c0 + width]

    z_ref[...] = proj(COL_Z, SSD_WIDTH)
    xbc_ref[...] = proj(COL_XBC, SSD_XBC)
    sc_ref[...] = proj(COL_SC, 3 * SC_WIDTH)

    row = i * tm + _iota((tm, LANES), 0)
    pos = (pos0 + (row & (seq_len - 1))).astype(F32)
    lane = _iota((tm, LANES), 1)
    ll = lane & (HEAD_DIM - 1)
    ang = pos * invf_ref[...]
    cos_a = jnp.cos(ang)
    sin_a = jnp.sin(ang)
    half = ROT_DIM // 2
    cos_t = jnp.where(ll < ROT_DIM, cos_a, 1.0)
    sin_t = jnp.where(ll < half, -sin_a, jnp.where(ll < ROT_DIM, sin_a, 0.0))

    q = proj(COL_Q, ATT_WIDTH)
    qraw_ref[...] = q.astype(BF16)
    qrope_ref[...] = jnp.concatenate(
        [_rope_chunk(q[:, c * LANES:(c + 1) * LANES], cos_t, sin_t, ll) for c in range(ATT_WIDTH // LANES)],
        axis=1).astype(BF16)

    kv = proj(COL_KV, 6 * KV_WIDTH)
    kcmp = kv[:, 0:LANES]
    vcmp = kv[:, LANES:2 * LANES]
    ksel = _rope_chunk(kv[:, 2 * LANES:3 * LANES], cos_t, sin_t, ll)
    vsel = kv[:, 3 * LANES:4 * LANES]
    kwin = _rope_chunk(kv[:, 4 * LANES:5 * LANES], cos_t, sin_t, ll)
    vwin = kv[:, 5 * LANES:6 * LANES]
    for ref, val in ((kcmp_ref, kcmp), (vcmp_ref, vcmp), (ksel_ref, ksel), (vsel_ref, vsel),
                     (kwin_ref, kwin), (vwin_ref, vwin)):
        if dims_major:
            ref[0, 0] = val.T
        else:
            ref[...] = val
    kcmp_rows_ref[...] = kcmp
    vcmp_rows_ref[...] = vcmp
    kselb_ref[...] = ksel.astype(BF16)
    vselb_ref[...] = vsel.astype(BF16)
    kwinb_ref[...] = kwin.astype(BF16)
    vwinb_ref[...] = vwin.astype(BF16)

    sm = proj(COL_SMALL, LANES)
    is_gate = (lane >= GATE_LANE0) & (lane < GATE_LANE0 + 3 * ATT_HEADS)
    small_ref[...] = jnp.where(is_gate, _sigmoid(sm), sm)


def _inproj(h, norm_w, w_perm, invf, *, batch, seq_len, pos0, layer, depth, carried):
    t = h.shape[0]
    tm = min(INPROJ_ROWS, t)
    assert t % tm == 0 and seq_len & (seq_len - 1) == 0
    dims_major = seq_len % tm == 0
    tiles_per_row = max(seq_len // tm, 1)
    row_spec = lambda w: pl.BlockSpec((tm, w), lambda i: (i, 0))
    const_spec = lambda a: pl.BlockSpec(a.shape, lambda i: (0, 0))
    row_out = lambda w, dt: (row_spec(w), jax.ShapeDtypeStruct((t, w), dt))
    if dims_major:
        kv_out = (pl.BlockSpec((1, 1, KV_WIDTH, tm), lambda i: (layer, i // tiles_per_row, 0, i % tiles_per_row)),
                  jax.ShapeDtypeStruct((depth, batch, KV_WIDTH, seq_len), F32))
    else:
        kv_out = row_out(KV_WIDTH, F32)
    outs = ([row_out(w, F32) for w in (SSD_WIDTH, SSD_XBC, 3 * SC_WIDTH)] + [row_out(ATT_WIDTH, BF16)] * 2
            + [kv_out] * 6 + [row_out(LANES, F32)] + [row_out(KV_WIDTH, F32)] * 2 + [row_out(KV_WIDTH, BF16)] * 4)
    first_kv_out = 5
    n_fixed_in = 4
    assert len(carried) in (0, 6) and (dims_major or not carried)
    return pl.pallas_call(
        functools.partial(_inproj_kernel, tm=tm, seq_len=seq_len, pos0=pos0, dims_major=dims_major,
                          n_carried=len(carried)),
        grid=(t // tm,),
        in_specs=([row_spec(D_MODEL), const_spec(norm_w), const_spec(w_perm), const_spec(invf)]
                  + [pl.BlockSpec(memory_space=pl.ANY)] * len(carried)),
        out_specs=[o[0] for o in outs],
        out_shape=[o[1] for o in outs],
        input_output_aliases={n_fixed_in + n: first_kv_out + n for n in range(len(carried))},
        scratch_shapes=[pltpu.VMEM((tm, IN_PAD), F32)],
        compiler_params=pltpu.CompilerParams(dimension_semantics=("arbitrary",), vmem_limit_bytes=VMEM_LIMIT_BYTES),
        name="inproj",
    )(h, norm_w, w_perm, invf, *carried)


def _pad_rows(x, rows):
    if x.shape[0] == rows:
        return x
    return jnp.concatenate([x, jnp.zeros((rows - x.shape[0], x.shape[1]), x.dtype)], axis=0)


def _mixer_kernel(z_ref, xbc_ref, sc_ref, small_ref, ssm0_ref, cssd0_ref, csc0_ref,
                  cw_ref, cb_ref, dtb_ref, alog_ref, dskip_ref, nw_ref, scw_ref, tril_ref, expand_ref,
                  yssd_ref, ysc_ref, ssm_out_ref, cssd_out_ref, csc_out_ref,
                  s_scr, xp_scr, scp_scr, *, cs, n_chunks):
    C = SSD_CHUNK
    c = pl.program_id(1)

    @pl.when(c == 0)
    def _():
        s_scr[...] = ssm0_ref[0]
        xp_scr[0:SUBLANES, :] = cssd0_ref[0]
        scp_scr[0:SUBLANES, :] = csc0_ref[0]

    xp_scr[SUBLANES:SUBLANES + C, :] = _pad_rows(xbc_ref[...], C)
    cw = cw_ref[...]
    conv = cb_ref[...]
    for k in range(SSD_CONV):
        off = SUBLANES - (SSD_CONV - 1) + k
        conv = conv + cw[k:k + 1, :] * xp_scr[off:off + C, :]
    xbc_c = _silu(conv)
    tail_ssd = xp_scr[cs:cs + SUBLANES, :]
    xp_scr[0:SUBLANES, :] = tail_ssd

    xs = xbc_c[:, 0:SSD_WIDTH]
    bm = xbc_c[:, SSD_WIDTH:SSD_WIDTH + SSD_GROUPS * SSD_STATE]
    cm = xbc_c[:, SSD_WIDTH + SSD_GROUPS * SSD_STATE:SSD_XBC]

    lane = _iota((C, LANES), 1)
    rowi = _iota((C, LANES), 0)
    small = _pad_rows(small_ref[...], C)
    dt_raw = small + dtb_ref[...]
    dt = jnp.maximum(dt_raw, 0.0) + jnp.log1p(jnp.exp(-jnp.abs(dt_raw)))
    dt = jnp.where((lane < SSD_HEADS) & (rowi < cs), dt, 0.0)
    a_row = jnp.where(lane[0:1, :] < SSD_HEADS, -jnp.exp(alog_ref[...]), 0.0)
    acum = _dot_exact(tril_ref[...], dt * a_row)
    acum_t = acum.T
    a_last = acum[C - 1:C, :]
    expand = expand_ref[...]

    def per_head_lanes(x):
        hi = x.astype(BF16)
        r1 = x - hi.astype(F32)
        mid = r1.astype(BF16)
        lo = (r1 - mid.astype(F32)).astype(BF16)
        return _dot(hi, expand) + _dot(mid, expand) + _dot(lo, expand)

    dt_w = per_head_lanes(dt)
    eacc_w = per_head_lanes(jnp.exp(acum))
    dend_w = per_head_lanes(jnp.exp(a_last - acum))
    cdec = jnp.exp(a_last)
    xdt = xs * dt_w
    xdtw_t = (xdt * dend_w).T

    causal = _iota((C, C), 0) >= _iota((C, C), 1)
    head_lo = lane < HEAD_DIM
    zeros_half = jnp.zeros((HEAD_DIM, SSD_STATE), BF16)
    y_chunks = [None] * (SSD_WIDTH // LANES)
    for g in range(SSD_GROUPS):
        bm_g = bm[:, g * SSD_STATE:(g + 1) * SSD_STATE].astype(BF16)
        cm_g = cm[:, g * SSD_STATE:(g + 1) * SSD_STATE].astype(BF16)
        cb = _dot_nt(cm_g, bm_g)
        for r in range(SSD_HEADS // SSD_GROUPS):
            h = g * (SSD_HEADS // SSD_GROUPS) + r
            ch, lo = h // 2, h % 2 == 0
            keep = head_lo if lo else jnp.logical_not(head_lo)
            seg = acum[:, h:h + 1] - acum_t[h:h + 1, :]
            decay = jnp.where(causal, jnp.exp(jnp.where(causal, seg, 0.0)), 0.0)
            xdt_h = jnp.where(keep, xdt[:, ch * LANES:(ch + 1) * LANES], 0.0).astype(BF16)
            y_d = _dot((cb * decay).astype(BF16), xdt_h)
            s_h = s_scr[h]
            s_b = s_h.astype(BF16)
            s_pl = jnp.concatenate([s_b, zeros_half] if lo else [zeros_half, s_b], axis=0)
            y_o = _dot_nt(cm_g, s_pl)
            y_h = y_d + eacc_w[:, ch * LANES:(ch + 1) * LANES] * y_o
            y_chunks[ch] = y_h if y_chunks[ch] is None else y_chunks[ch] + y_h
            s_scr[h] = s_h * cdec[:, h:h + 1] + _dot(xdtw_t[h * HEAD_DIM:(h + 1) * HEAD_DIM, :].astype(BF16), bm_g)
    y = jnp.concatenate(y_chunks, axis=1) + dskip_ref[...] * xs
    gated = y * _silu(_pad_rows(z_ref[...], C))
    yssd_ref[...] = _rms(gated, nw_ref[...])[0:cs]

    sc = _pad_rows(sc_ref[...], C)
    scp_scr[SUBLANES:SUBLANES + C, :] = sc[:, SC_WIDTH:2 * SC_WIDTH] * sc[:, 2 * SC_WIDTH:3 * SC_WIDTH]
    scw = scw_ref[...]
    conv3 = jnp.zeros((C, SC_WIDTH), F32)
    for k in range(SC_CONV):
        off = SUBLANES - (SC_CONV - 1) + k
        conv3 = conv3 + scw[k:k + 1, :] * scp_scr[off:off + C, :]
    ysc_ref[...] = (sc[:, 0:SC_WIDTH] * conv3)[0:cs]
    tail_sc = scp_scr[cs:cs + SUBLANES, :]
    scp_scr[0:SUBLANES, :] = tail_sc

    @pl.when(c == n_chunks - 1)
    def _():
        ssm_out_ref[0] = s_scr[...]
        cssd_out_ref[0] = tail_ssd
        csc_out_ref[0] = tail_sc


def _mixer(z, xbc, sc, small, ssm0, cssd0, csc0, lw, *, batch, seq_len):
    cs = min(SSD_CHUNK, seq_len)
    assert seq_len % cs == 0 and cs % SUBLANES == 0
    nc = seq_len // cs
    t = batch * seq_len
    row_spec = lambda w: pl.BlockSpec((cs, w), lambda b, c: (b * nc + c, 0))
    const_spec = lambda a: pl.BlockSpec(a.shape, lambda b, c: (0,) * a.ndim)
    batch_spec = lambda a: pl.BlockSpec((1,) + a.shape[1:], lambda b, c: (b,) + (0,) * (a.ndim - 1))
    consts = [lw["ssd_conv_w"], lw["ssd_conv_b"], lw["dt_bias"], lw["a_log"], lw["d_skip"], lw["ssd_norm_w"],
              lw["sc_conv_w"], lw["tril"], lw["expand"]]
    out_shape = [jax.ShapeDtypeStruct((t, SSD_WIDTH), F32), jax.ShapeDtypeStruct((t, SC_WIDTH), F32),
                 jax.ShapeDtypeStruct(ssm0.shape, F32), jax.ShapeDtypeStruct(cssd0.shape, F32),
                 jax.ShapeDtypeStruct(csc0.shape, F32)]
    return pl.pallas_call(
        functools.partial(_mixer_kernel, cs=cs, n_chunks=nc),
        grid=(batch, nc),
        in_specs=[row_spec(SSD_WIDTH), row_spec(SSD_XBC), row_spec(3 * SC_WIDTH), row_spec(LANES),
                  batch_spec(ssm0), batch_spec(cssd0), batch_spec(csc0)] + [const_spec(a) for a in consts],
        out_specs=[row_spec(SSD_WIDTH), row_spec(SC_WIDTH), batch_spec(ssm0), batch_spec(cssd0), batch_spec(csc0)],
        out_shape=out_shape,
        scratch_shapes=[pltpu.VMEM((SSD_HEADS, HEAD_DIM, SSD_STATE), F32),
                        pltpu.VMEM((SUBLANES + SSD_CHUNK, SSD_XBC), F32),
                        pltpu.VMEM((SUBLANES + SSD_CHUNK, SC_WIDTH), F32)],
        compiler_params=pltpu.CompilerParams(dimension_semantics=("arbitrary", "arbitrary"),
                                             vmem_limit_bytes=VMEM_LIMIT_BYTES),
        name="mixer",
    )(z, xbc, sc, small, ssm0, cssd0, csc0, *consts)


def _compress_hidden(read_rows, n_units, pe, wr):
    units = jnp.concatenate([read_rows(r).astype(BF16) for r in range(CMP_STRIDE)], axis=1)
    acc = _dot(units, wr)
    pe_rows = jnp.concatenate([jnp.broadcast_to(pe[h:h + 1, :], (SUBLANES // 2, pe.shape[1])) for h in range(2)], axis=0)
    pe_both = _dot(pe_rows.astype(BF16), wr)
    pe_term = pe_both[0:1, 0:KV_WIDTH] + pe_both[SUBLANES // 2:SUBLANES // 2 + 1, KV_WIDTH:2 * KV_WIDTH]
    return _silu(acc[:, 0:KV_WIDTH] + pltpu.roll(acc[:, KV_WIDTH:2 * KV_WIDTH], n_units - 1, 0) + pe_term)


def _compress_prompt_kernel(kc_ref, vc_ref, pe_ref, wr_ref, w2_ref, w2t_ref, ck_ref, cvt_ref, *, n_units):
    hid_k = _compress_hidden(lambda r: kc_ref[pl.ds(r, n_units, stride=CMP_STRIDE), :], n_units, pe_ref[0], wr_ref[0])
    ck_ref[0] = _dot(hid_k.astype(BF16), w2_ref[0]).astype(BF16)
    hid_v = _compress_hidden(lambda r: vc_ref[pl.ds(r, n_units, stride=CMP_STRIDE), :], n_units, pe_ref[1], wr_ref[1])
    cvt_ref[0] = _dot(w2t_ref[1], hid_v.T.astype(BF16)).astype(BF16)


def _compress_prompt(kc, vc, lw, *, batch, seq_len):
    n_units = seq_len // CMP_STRIDE
    consts = [lw["cmp_pe"], lw["cmp_wr"], lw["cmp_w2"], lw["cmp_w2t"]]
    const_spec = lambda a: pl.BlockSpec(a.shape, lambda b: (0,) * a.ndim)
    in_spec = pl.BlockSpec((seq_len, KV_WIDTH), lambda b: (b, 0))
    return pl.pallas_call(
        functools.partial(_compress_prompt_kernel, n_units=n_units),
        grid=(batch,),
        in_specs=[in_spec, in_spec] + [const_spec(a) for a in consts],
        out_specs=[pl.BlockSpec((1, n_units, KV_WIDTH), lambda b: (b, 0, 0)),
                   pl.BlockSpec((1, KV_WIDTH, n_units), lambda b: (b, 0, 0))],
        out_shape=[jax.ShapeDtypeStruct((batch, n_units, KV_WIDTH), BF16),
                   jax.ShapeDtypeStruct((batch, KV_WIDTH, n_units), BF16)],
        compiler_params=pltpu.CompilerParams(dimension_semantics=("arbitrary",), vmem_limit_bytes=VMEM_LIMIT_BYTES),
        name="compress_prompt",
    )(kc, vc, *consts)


def _page_copy(cache_hbm, layer, page, buf, slot, p, sem):
    return pltpu.make_async_copy(cache_hbm.at[layer, page], buf.at[slot, p], sem)


def _pages_dims_major(buf, slot, first_page, n):
    return jnp.concatenate([buf[slot, first_page + p] for p in range(n)], axis=1)


def _gather_pages(pt_ref, b, cache_hbm, layer, buf, slot, sem, n_pages):
    def start(p, carry):
        _page_copy(cache_hbm, layer, pt_ref[b, p], buf, slot, p, sem).start()
        return carry
    lax.fori_loop(0, n_pages, start, 0)


def _wait_pages(cache_hbm, layer, buf, slot, sem, n_pages):
    def wait(p, carry):
        _page_copy(cache_hbm, layer, 0, buf, slot, p, sem).wait()
        return carry
    lax.fori_loop(0, n_pages, wait, 0)


def _prefetch_pages(pt_ref, cache_hbm, layer, buf, sems, n_pages):
    b = pl.program_id(0)
    slot = b % 2

    @pl.when(b == 0)
    def _():
        _gather_pages(pt_ref, 0, cache_hbm, layer, buf, 0, sems.at[0], n_pages)

    @pl.when(b + 1 < pl.num_programs(0))
    def _():
        _gather_pages(pt_ref, b + 1, cache_hbm, layer, buf, 1 - slot, sems.at[1 - slot], n_pages)
    return slot


def _compress_paged_kernel(pt_ref, kc_hbm, vc_hbm, pe_ref, wr_ref, w2_ref, ck_ref, cv_ref,
                           kbuf, vbuf, rows_scr, ksems, vsems, *, layer, n_pages):
    n_units = n_pages * (PAGE_SIZE // CMP_STRIDE)
    slot = _prefetch_pages(pt_ref, kc_hbm, layer, kbuf, ksems, n_pages)
    _prefetch_pages(pt_ref, vc_hbm, layer, vbuf, vsems, n_pages)
    for idx, (src_hbm, buf, sems, dst) in enumerate(((kc_hbm, kbuf, ksems, ck_ref), (vc_hbm, vbuf, vsems, cv_ref))):
        _wait_pages(src_hbm, layer, buf, slot, sems.at[slot], n_pages)

        def to_rows(c, carry, buf=buf):
            for p in range(pages_per_trip):
                page = c * pages_per_trip + p
                rows_scr[pl.ds(pl.multiple_of(page * PAGE_SIZE, PAGE_SIZE), PAGE_SIZE), :] = buf[slot, page].T
            return carry
        pages_per_trip = math.gcd(n_pages, 8)
        lax.fori_loop(0, n_pages // pages_per_trip, to_rows, 0)

        hid = _compress_hidden(lambda r: rows_scr[pl.ds(r, n_units, stride=CMP_STRIDE), :], n_units,
                               pe_ref[idx], wr_ref[idx])
        dst[0] = _dot(hid.astype(BF16), w2_ref[idx]).astype(BF16)


def _compress_paged(page_table, cache_kt, cache_vt, lw, *, layer):
    batch, n_pages = page_table.shape
    past = n_pages * PAGE_SIZE
    n_units = past // CMP_STRIDE
    consts = [lw["cmp_pe"], lw["cmp_wr"], lw["cmp_w2"]]
    const_spec = lambda a: pl.BlockSpec(a.shape, lambda b, pt: (0,) * a.ndim)
    any_spec = pl.BlockSpec(memory_space=pl.ANY)
    out_spec = pl.BlockSpec((1, n_units, KV_WIDTH), lambda b, pt: (b, 0, 0))
    grid_spec = pltpu.PrefetchScalarGridSpec(
        num_scalar_prefetch=1, grid=(batch,),
        in_specs=[any_spec, any_spec] + [const_spec(a) for a in consts],
        out_specs=[out_spec, out_spec],
        scratch_shapes=[pltpu.VMEM((2, n_pages, KV_WIDTH, PAGE_SIZE), F32), pltpu.VMEM((2, n_pages, KV_WIDTH, PAGE_SIZE), F32),
                        pltpu.VMEM((past, KV_WIDTH), F32),
                        pltpu.SemaphoreType.DMA((2,)), pltpu.SemaphoreType.DMA((2,))])
    return pl.pallas_call(
        functools.partial(_compress_paged_kernel, layer=layer, n_pages=n_pages),
        grid_spec=grid_spec,
        out_shape=[jax.ShapeDtypeStruct((batch, n_units, KV_WIDTH), BF16)] * 2,
        compiler_params=pltpu.CompilerParams(dimension_semantics=("arbitrary",), vmem_limit_bytes=VMEM_LIMIT_BYTES),
        name="compress_paged",
    )(page_table, cache_kt, cache_vt, *consts)


def _stack_heads(q):
    lane = _iota((q.shape[0], LANES), 1)
    lo = lane < HEAD_DIM
    c0, c1, c2 = (q[:, c * LANES:(c + 1) * LANES] for c in range(3))
    blocks = [jnp.where(lo, c0, 0.0), jnp.where(lo, pltpu.roll(c0, HEAD_DIM, 1), 0.0), jnp.where(lo, c1, 0.0),
              jnp.where(lo, 0.0, c1), jnp.where(lo, 0.0, pltpu.roll(c2, HEAD_DIM, 1)), jnp.where(lo, 0.0, c2)]
    return jnp.concatenate(blocks, axis=0)


def _unstack_heads(o, rows):
    lane = _iota((rows, LANES), 1)
    lo = lane < HEAD_DIM
    blk = [o[j * rows:(j + 1) * rows] for j in range(ATT_HEADS)]
    return jnp.concatenate([jnp.where(lo, blk[0], pltpu.roll(blk[1], HEAD_DIM, 1)),
                            jnp.where(lo, blk[2], blk[3]),
                            jnp.where(lo, pltpu.roll(blk[4], HEAD_DIM, 1), blk[5])], axis=1)


def _tile_heads(x, reps):
    return jnp.concatenate([x] * reps, axis=0)


def _masked_softmax(s, mask):
    s = jnp.where(mask, s, NEG)
    e = jnp.where(mask, jnp.exp(s - jnp.max(s, axis=1, keepdims=True)), 0.0)
    l = jnp.sum(e, axis=1, keepdims=True)
    return e * (1.0 / jnp.where(l > 0.0, l, 1.0))


def _split_bf16(x):
    hi = x.astype(BF16)
    return hi, (x - hi.astype(F32)).astype(BF16)


def _select_blocks(imp, tq, axis):
    n_blocks = imp.shape[axis]
    blk = _iota(imp.shape, axis)
    cur = tq >> 6
    causal = blk * SEL_BLOCK <= tq
    forced = causal & ((blk == 0) | (blk == cur) | (blk == cur - 1))
    score = jnp.where(forced, FORCE, jnp.where(causal, imp, -FORCE))
    blk_f = blk.astype(F32)
    for _ in range(N_SEL):
        mx = jnp.max(score, axis=axis, keepdims=True)
        first = jnp.min(jnp.where(score == mx, blk_f, float(n_blocks)), axis=axis, keepdims=True)
        score = jnp.where(blk_f == first, -jnp.inf, score)
    return jnp.where((score == -jnp.inf) & causal, 1.0, 0.0)


def _combine_gates(gates, o_c, o_s, o_w, rows):
    out = []
    for hh in range(ATT_HEADS):
        sl = slice(hh * rows, (hh + 1) * rows)
        g0 = GATE_LANE0 + 3 * hh
        out.append(gates[:, g0:g0 + 1] * o_c[sl] + gates[:, g0 + 1:g0 + 2] * o_s[sl] + gates[:, g0 + 2:g0 + 3] * o_w[sl])
    return jnp.concatenate(out, axis=0)


def _attn_prompt_kernel(qr_ref, qn_ref, small_ref, ck_ref, cvt_ref, ks_ref, vs_ref, kw_ref, vw_ref,
                        onehot_ref, mimp_ref, out_ref,
                        qaug_scr, sa_scr, sb_scr, p_scr, m_scr, l_scr, a_scr, acc_scr,
                        sct_scr, pct_scr, psum_scr, sw_scr, pw_scr, lw_scr, oc_scr, ow_scr):
    R = ATTN_ROWS
    TK = KEYS_PER_TILE
    i = pl.program_id(1)
    t0 = pl.multiple_of(i * R, R)
    scale = HEAD_DIM ** -0.5 * math.log2(math.e)
    tq = t0 + _iota((R, 1), 0)
    tq_row = t0 + _iota((1, R), 1)
    q_rope = _stack_heads(qr_ref[...].astype(F32) * scale).astype(BF16)
    q_raw = _stack_heads(qn_ref[...].astype(F32) * scale).astype(BF16)
    qaug_scr[:, 0:LANES] = q_rope

    span = WINDOW + R
    s0 = pl.multiple_of(jnp.maximum(t0 - WINDOW, 0), R)
    sw_scr[...] = _dot_nt(q_rope, kw_ref[pl.ds(s0, span), :])
    spos = s0 + _iota((1, span), 1)
    win_bias = jnp.where((spos <= tq) & (spos > tq - WINDOW), 0.0, NEG)
    for j in range(ATT_HEADS):
        rows = slice(j * R, (j + 1) * R)
        s = sw_scr[rows, :] + win_bias
        e = jnp.exp2(s - jnp.max(s, axis=1, keepdims=True))
        pw_scr[rows, :] = e.astype(BF16)
        lw_scr[rows, :] = jnp.broadcast_to(jnp.sum(e, axis=1, keepdims=True), (R, LANES))
    ow_scr[...] = _dot(pw_scr[...], vw_ref[pl.ds(s0, span), :]) * (1.0 / lw_scr[...])

    n_units = ck_ref.shape[1]
    sct_scr[...] = _dot_nt(ck_ref[0], q_raw)
    cmp_end = _iota((n_units, 1), 0) * CMP_STRIDE + (CMP_LEN - 1)
    cmp_bias = jnp.where(cmp_end <= tq_row, 0.0, NEG)
    for j in range(ATT_HEADS):
        cols = slice(j * R, (j + 1) * R)
        s = sct_scr[:, cols] + cmp_bias
        mx = jnp.max(s, axis=0, keepdims=True)
        e = jnp.exp2(s - mx)
        inv = jnp.where(mx > NEG / 2, 1.0 / jnp.sum(e, axis=0, keepdims=True), 0.0)
        p = e * inv
        pct_scr[:, cols] = p.astype(BF16)
        k, g = divmod(j, GQA)
        if g == 0:
            psum_scr[k] = p
        else:
            psum_scr[k] += p
    o_c_t = _dot(cvt_ref[0], pct_scr[...])
    oc_scr[...] = jnp.concatenate([o_c_t[:, j * R:(j + 1) * R].T for j in range(ATT_HEADS)], axis=0)

    for k in range(ATT_KV_HEADS):
        hi, lo = _split_bf16(psum_scr[k])
        imp_t = _dot(mimp_ref[...], hi) + _dot(mimp_ref[...], lo)
        sel_t = _select_blocks(imp_t, tq_row, 0)
        bias = jnp.where(sel_t.T > 0.5, 0.0, NEG).astype(BF16)
        for g in range(GQA):
            j = k * GQA + g
            qaug_scr[j * R:(j + 1) * R, LANES:2 * LANES] = bias

    def scores(k0, s_ref):
        k_aug = jnp.concatenate([ks_ref[pl.ds(k0, TK), :], onehot_ref[pl.ds(k0, TK), :]], axis=1)
        s_ref[...] = _dot_nt(qaug_scr[...], k_aug)

    def accumulate(k0, s_ref, first):
        if first:
            causal_bias = jnp.where((k0 + _iota((1, TK), 1)) <= tq, 0.0, NEG)
        for j in range(ATT_HEADS):
            rows = slice(j * R, (j + 1) * R)
            s = s_ref[rows, :]
            if first:
                s = s + causal_bias
                m_new = jnp.broadcast_to(jnp.max(s, axis=1, keepdims=True), (R, LANES))
            else:
                m_old = m_scr[rows, :]
                m_new = jnp.maximum(m_old, jnp.max(s, axis=1, keepdims=True))
                alpha = jnp.exp2(m_old - m_new)
            p = jnp.exp2(s - jnp.concatenate([m_new] * (TK // LANES), axis=1))
            l_new = jnp.sum(p, axis=1, keepdims=True)
            p_scr[rows, :] = p.astype(BF16)
            m_scr[rows, :] = m_new
            if first:
                l_scr[rows, :] = jnp.broadcast_to(l_new, (R, LANES))
            else:
                l_scr[rows, :] = alpha * l_scr[rows, :] + l_new
                a_scr[rows, :] = alpha
        pv = _dot(p_scr[...], vs_ref[pl.ds(k0, TK), :])
        if first:
            acc_scr[...] = pv
        else:
            acc_scr[...] = a_scr[...] * acc_scr[...] + pv

    n_past = t0 // TK
    k_diag = pl.multiple_of(n_past * TK, TK)
    scores(k_diag, sa_scr)
    accumulate(k_diag, sa_scr, True)

    @pl.when(n_past % 2 == 1)
    def _():
        k_odd = pl.multiple_of((n_past - 1) * TK, TK)
        scores(k_odd, sb_scr)
        accumulate(k_odd, sb_scr, False)

    n_pairs = n_past // 2

    @pl.when(n_pairs > 0)
    def _():
        scores(0, sa_scr)

        def pair_body(jj, carry):
            k_even = pl.multiple_of(2 * jj * TK, TK)
            k_odd = pl.multiple_of(k_even + TK, TK)
            scores(k_odd, sb_scr)
            accumulate(k_even, sa_scr, False)
            scores(pl.multiple_of(k_odd + TK, TK), sa_scr)
            accumulate(k_odd, sb_scr, False)
            return carry
        lax.fori_loop(0, n_pairs - 1, pair_body, 0)
        k_even = pl.multiple_of(2 * (n_pairs - 1) * TK, TK)
        k_odd = pl.multiple_of(k_even + TK, TK)
        scores(k_odd, sb_scr)
        accumulate(k_even, sa_scr, False)
        accumulate(k_odd, sb_scr, False)
    o_s = acc_scr[...] * (1.0 / l_scr[...])

    out_ref[...] = _unstack_heads(_combine_gates(small_ref[...], oc_scr[...], o_s, ow_scr[...], R), R)


def _attn_prompt(qrope, qraw, small, ck, cvt, kselb, vselb, kwinb, vwinb, *, batch, seq_len):
    span = WINDOW + ATTN_ROWS
    assert seq_len % KEYS_PER_TILE == 0 and seq_len >= span and seq_len // SEL_BLOCK <= LANES
    assert KEYS_PER_TILE % ATTN_ROWS == 0
    nqb = seq_len // ATTN_ROWS
    n_units = seq_len // CMP_STRIDE
    t = batch * seq_len
    pos = jnp.arange(seq_len, dtype=jnp.int32)
    onehot = (pos[:, None] // SEL_BLOCK == jnp.arange(LANES, dtype=jnp.int32)[None, :]).astype(BF16)
    m_imp_t = _importance_matrix(n_units, LANES).T
    q_spec = lambda w: pl.BlockSpec((ATTN_ROWS, w), lambda b, i: (b * nqb + i, 0))
    batch_spec = lambda a: pl.BlockSpec((1,) + a.shape[1:], lambda b, i: (b, 0, 0))
    seq_spec = pl.BlockSpec((seq_len, KV_WIDTH), lambda b, i: (b, 0))
    const_spec = lambda a: pl.BlockSpec(a.shape, lambda b, i: (0, 0))
    rows = ATT_HEADS * ATTN_ROWS
    vmem = pltpu.VMEM
    scratch = [vmem((rows, 2 * LANES), BF16),
               vmem((rows, KEYS_PER_TILE), F32), vmem((rows, KEYS_PER_TILE), F32),
               vmem((rows, KEYS_PER_TILE), BF16),
               vmem((rows, LANES), F32), vmem((rows, LANES), F32), vmem((rows, LANES), F32),
               vmem((rows, LANES), F32),
               vmem((n_units, rows), F32), vmem((n_units, rows), BF16),
               vmem((ATT_KV_HEADS, n_units, ATTN_ROWS), F32),
               vmem((rows, span), F32), vmem((rows, span), BF16), vmem((rows, LANES), F32),
               vmem((rows, LANES), F32), vmem((rows, LANES), F32)]
    return pl.pallas_call(
        _attn_prompt_kernel,
        grid=(batch, nqb),
        in_specs=[q_spec(ATT_WIDTH), q_spec(ATT_WIDTH), q_spec(LANES), batch_spec(ck), batch_spec(cvt),
                  seq_spec, seq_spec, seq_spec, seq_spec, const_spec(onehot), const_spec(m_imp_t)],
        out_specs=q_spec(ATT_WIDTH),
        out_shape=jax.ShapeDtypeStruct((t, ATT_WIDTH), F32),
        scratch_shapes=scratch,
        compiler_params=pltpu.CompilerParams(dimension_semantics=("arbitrary", "arbitrary"),
                                             vmem_limit_bytes=VMEM_LIMIT_BYTES),
        name="attn_prompt",
    )(qrope, qraw, small, ck, cvt, kselb, vselb, kwinb, vwinb, onehot, m_imp_t)


def _importance_matrix(n_units, width):
    n = jnp.arange(n_units, dtype=jnp.int32)[:, None]
    j = jnp.arange(width, dtype=jnp.int32)[None, :]
    upb = SEL_BLOCK // CMP_STRIDE
    m = (n // upb == j).astype(F32) + ((n + 1) // upb == j).astype(F32)
    return jnp.where(n < n_units - 1, m, 0.0).astype(BF16)


def _attn_sample_kernel(pt_ref, ks_hbm, vs_hbm, qr_ref, qn_ref, small_ref, ck_ref, cv_ref,
                        knew_ref, vnew_ref, kwc_ref, vwc_ref, kwnew_ref, vwnew_ref, onehot_ref, mimp_ref,
                        out_ref, kbuf, vbuf, ksems, vsems, *, layer, n_pages, past, win_buf, group_blocks):
    R = qr_ref.shape[0]
    slot = _prefetch_pages(pt_ref, ks_hbm, layer, kbuf, ksems, n_pages)
    _prefetch_pages(pt_ref, vs_hbm, layer, vbuf, vsems, n_pages)

    scale = HEAD_DIM ** -0.5
    tq = past + _iota((R, 1), 0)
    tq_all = _tile_heads(tq, ATT_HEADS)
    q_rope = _stack_heads(qr_ref[...].astype(F32) * scale).astype(BF16)
    q_raw = _stack_heads(qn_ref[...].astype(F32) * scale).astype(BF16)

    n_units = ck_ref.shape[1]
    s_c = _dot_nt(q_raw, ck_ref[0])
    cmp_end = _iota((1, n_units), 1) * CMP_STRIDE + (CMP_LEN - 1)
    p_c = _masked_softmax(s_c, cmp_end <= tq_all)
    o_c = _dot(p_c.astype(BF16), cv_ref[0])

    bias = []
    for k in range(ATT_KV_HEADS):
        hi, lo = _split_bf16(sum(p_c[(k * GQA + g) * R:(k * GQA + g + 1) * R] for g in range(GQA)))
        sel = _select_blocks(_dot(hi, mimp_ref[...]) + _dot(lo, mimp_ref[...]), tq, 1)
        bias.append(_tile_heads(jnp.where(sel > 0.5, 0.0, NEG), GQA))
    bias = jnp.concatenate(bias, axis=0).astype(BF16)

    k_off = past - win_buf
    s_wc = _dot(q_rope, kwc_ref[0, 0].astype(BF16))
    s_wn = _dot_nt(q_rope, kwnew_ref[...])
    pos_c = k_off + _iota((1, win_buf), 1)
    pos_n = past + _iota((1, R), 1)
    mask_c = (pos_c <= tq_all) & (pos_c > tq_all - WINDOW) & (pos_c >= k_off)
    mask_n = (pos_n <= tq_all) & (pos_n > tq_all - WINDOW) & (pos_n >= k_off)
    s_wc = jnp.where(mask_c, s_wc, NEG)
    s_wn = jnp.where(mask_n, s_wn, NEG)
    m_w = jnp.maximum(jnp.max(s_wc, axis=1, keepdims=True), jnp.max(s_wn, axis=1, keepdims=True))
    e_wc = jnp.where(mask_c, jnp.exp(s_wc - m_w), 0.0)
    e_wn = jnp.where(mask_n, jnp.exp(s_wn - m_w), 0.0)
    l_w = jnp.sum(e_wc, axis=1, keepdims=True) + jnp.sum(e_wn, axis=1, keepdims=True)
    o_w = _dot_nt(e_wc.astype(BF16), vwc_ref[0, 0].astype(BF16)) + _dot(e_wn.astype(BF16), vwnew_ref[...])
    o_w = o_w * (1.0 / jnp.where(l_w > 0.0, l_w, 1.0))

    s_n = _dot_nt(q_rope, knew_ref[...])
    mask_sn = pos_n <= tq_all
    s_n = jnp.where(mask_sn, s_n, NEG)
    _wait_pages(ks_hbm, layer, kbuf, slot, ksems.at[slot], n_pages)
    group_keys = group_blocks * SEL_BLOCK
    n_groups = (n_pages * PAGE_SIZE) // group_keys
    s_past = []
    group_pages = group_keys // PAGE_SIZE
    for gi in range(n_groups):
        q_aug = jnp.concatenate([q_rope, bias[:, gi * group_blocks:(gi + 1) * group_blocks]], axis=1)
        k_group = _pages_dims_major(kbuf, slot, gi * group_pages, group_pages)
        k_aug_t = jnp.concatenate([k_group.astype(BF16), onehot_ref[...]], axis=0)
        s_past.append(_dot(q_aug, k_aug_t))
    m_s = jnp.max(s_n, axis=1, keepdims=True)
    for s in s_past:
        m_s = jnp.maximum(m_s, jnp.max(s, axis=1, keepdims=True))
    e_n = jnp.where(mask_sn, jnp.exp(s_n - m_s), 0.0)
    l_s = jnp.sum(e_n, axis=1, keepdims=True)
    o_s = _dot(e_n.astype(BF16), vnew_ref[...])
    _wait_pages(vs_hbm, layer, vbuf, slot, vsems.at[slot], n_pages)
    for gi, s in enumerate(s_past):
        e = jnp.exp(s - m_s)
        l_s = l_s + jnp.sum(e, axis=1, keepdims=True)
        v_group = _pages_dims_major(vbuf, slot, gi * group_pages, group_pages)
        o_s = o_s + _dot_nt(e.astype(BF16), v_group.astype(BF16))
    o_s = o_s * (1.0 / l_s)

    out_ref[...] = _unstack_heads(_combine_gates(small_ref[...], o_c, o_s, o_w, R), R)


def _attn_sample(page_table, cache_kst, cache_vst, cache_kwt, cache_vwt, qrope, qraw, small, ck, cv,
                 knew, vnew, kwnew, vwnew, *, layer, seq_len):
    batch, n_pages = page_table.shape
    past = n_pages * PAGE_SIZE
    win_buf = cache_kwt.shape[3]
    assert seq_len == SUBLANES and (past + seq_len) // CMP_STRIDE == past // CMP_STRIDE and past >= WINDOW
    n_units = past // CMP_STRIDE
    nb_past = past // SEL_BLOCK
    group_blocks = min(LANES, nb_past)
    assert nb_past % group_blocks == 0
    n_blk_lanes = -(-(nb_past + 1) // LANES) * LANES
    key = jnp.arange(group_blocks * SEL_BLOCK, dtype=jnp.int32)
    onehot = (key[None, :] // SEL_BLOCK == jnp.arange(group_blocks, dtype=jnp.int32)[:, None]).astype(BF16)
    m_imp = _importance_matrix(n_units, n_blk_lanes)
    win_spec = pl.BlockSpec((1, 1, KV_WIDTH, win_buf), lambda b, pt: (layer, b, 0, 0))
    any_spec = pl.BlockSpec(memory_space=pl.ANY)
    row_spec = lambda w: pl.BlockSpec((seq_len, w), lambda b, pt: (b, 0))
    batch_spec = lambda a: pl.BlockSpec((1,) + a.shape[1:], lambda b, pt: (b, 0, 0))
    const_spec = lambda a: pl.BlockSpec(a.shape, lambda b, pt: (0, 0))
    grid_spec = pltpu.PrefetchScalarGridSpec(
        num_scalar_prefetch=1, grid=(batch,),
        in_specs=[any_spec, any_spec, row_spec(ATT_WIDTH), row_spec(ATT_WIDTH), row_spec(LANES),
                  batch_spec(ck), batch_spec(cv), row_spec(KV_WIDTH), row_spec(KV_WIDTH),
                  win_spec, win_spec, row_spec(KV_WIDTH), row_spec(KV_WIDTH),
                  const_spec(onehot), const_spec(m_imp)],
        out_specs=row_spec(ATT_WIDTH),
        scratch_shapes=[pltpu.VMEM((2, n_pages, KV_WIDTH, PAGE_SIZE), F32), pltpu.VMEM((2, n_pages, KV_WIDTH, PAGE_SIZE), F32),
                        pltpu.SemaphoreType.DMA((2,)), pltpu.SemaphoreType.DMA((2,))])
    return pl.pallas_call(
        functools.partial(_attn_sample_kernel, layer=layer, n_pages=n_pages, past=past, win_buf=win_buf,
                          group_blocks=group_blocks),
        grid_spec=grid_spec,
        out_shape=jax.ShapeDtypeStruct((batch * seq_len, ATT_WIDTH), F32),
        compiler_params=pltpu.CompilerParams(dimension_semantics=("arbitrary",), vmem_limit_bytes=VMEM_LIMIT_BYTES),
        name="attn_sample",
    )(page_table, cache_kst, cache_vst, qrope, qraw, small, ck, cv, knew, vnew, cache_kwt, cache_vwt, kwnew, vwnew,
      onehot, m_imp)


def _ffn_kernel(h_ref, yssd_ref, ysc_ref, yatt_ref, n2_ref, fn_ref, wo_hbm, w1_hbm, w2_hbm, out_ref,
                wo_scr, w1_scr, w2_scr, hid_scr, sems, *, final):
    @pl.when(pl.program_id(0) == 0)
    def _():
        copies = [pltpu.make_async_copy(src, dst, sems.at[n])
                  for n, (src, dst) in enumerate(((wo_hbm, wo_scr), (w1_hbm, w1_scr), (w2_hbm, w2_scr)))]
        for c in copies:
            c.start()
        for c in copies:
            c.wait()

    mix = (_dot(yssd_ref[...].astype(BF16), wo_scr[0:SSD_WIDTH, :])
           + _dot(ysc_ref[...].astype(BF16), wo_scr[SSD_WIDTH:SSD_WIDTH + SC_WIDTH, :])
           + _dot(yatt_ref[...].astype(BF16), wo_scr[SSD_WIDTH + SC_WIDTH:D_MODEL, :]))
    hn = h_ref[...] + mix
    xn = _rms(hn, n2_ref[...]).astype(BF16)
    hid_scr[...] = jnp.square(jnp.maximum(_dot(xn, w1_scr[...]), 0.0)).astype(BF16)
    o = hn + _dot(hid_scr[...], w2_scr[...])
    out_ref[...] = _rms(o, fn_ref[...]) if final else o


def _ffn(h, yssd, ysc, yatt, lw, final_norm_w, *, final):
    t = h.shape[0]
    tm = min(FFN_ROWS, t)
    assert t % tm == 0
    row_spec = lambda w: pl.BlockSpec((tm, w), lambda i: (i, 0))
    const_spec = lambda a: pl.BlockSpec(a.shape, lambda i: (0, 0))
    any_spec = pl.BlockSpec(memory_space=pl.ANY)
    return pl.pallas_call(
        functools.partial(_ffn_kernel, final=final),
        grid=(t // tm,),
        in_specs=[row_spec(D_MODEL), row_spec(SSD_WIDTH), row_spec(SC_WIDTH), row_spec(ATT_WIDTH),
                  const_spec(lw["norm2_w"]), const_spec(final_norm_w), any_spec, any_spec, any_spec],
        out_specs=row_spec(D_MODEL),
        out_shape=jax.ShapeDtypeStruct((t, D_MODEL), F32),
        scratch_shapes=[pltpu.VMEM((D_MODEL, D_MODEL), BF16), pltpu.VMEM((D_MODEL, D_FF), BF16),
                        pltpu.VMEM((D_FF, D_MODEL), BF16), pltpu.VMEM((tm, D_FF), BF16),
                        pltpu.SemaphoreType.DMA((3,))],
        compiler_params=pltpu.CompilerParams(dimension_semantics=("arbitrary",), vmem_limit_bytes=VMEM_LIMIT_BYTES),
        name="ffn",
    )(h, yssd, ysc, yatt, lw["norm2_w"], final_norm_w, lw["w_out"], lw["w_ff1"], lw["w_ff2"])


def _lane_row(v, width):
    v = v.reshape(1, -1).astype(F32)
    return jnp.pad(v, ((0, 0), (0, width - v.shape[1])))


def _prep_layer(l, norm1_w, w_in, ssd_conv_w, ssd_conv_b, ssd_dt_bias, ssd_a_log, ssd_d, ssd_norm_w, sc_conv_w,
                cmp_pe, cmp_w1, cmp_w2, w_out, norm2_w, w_ff1, w_ff2):
    offs = [0]
    for s in IN_SIZES:
        offs.append(offs[-1] + s)
    seg = lambda a: jnp.arange(offs[a], offs[a + 1])
    perm = jnp.concatenate([seg(0), seg(1), seg(3), seg(4), seg(5), seg(6)] + [seg(a) for a in range(7, 13)]
                           + [seg(2), seg(13)])
    w_perm = jnp.pad(w_in[l][:, perm], ((0, 0), (0, IN_PAD - offs[-1]))).astype(BF16)

    eye = jnp.eye(ATT_KV_HEADS, dtype=F32)
    w1 = cmp_w1[l].reshape(2, 2, CMP_STRIDE, HEAD_DIM, HEAD_DIM)
    wfs = jnp.einsum('vhrde,kK->vhrkdKe', w1, eye).reshape(2, 2, CMP_STRIDE * KV_WIDTH, KV_WIDTH).astype(BF16)
    pe = cmp_pe[l].reshape(2, 2, CMP_STRIDE, 1, HEAD_DIM)
    pe = jnp.broadcast_to(pe, (2, 2, CMP_STRIDE, ATT_KV_HEADS, HEAD_DIM)).reshape(2, 2, CMP_STRIDE * KV_WIDTH)
    w2 = jnp.einsum('vde,kK->vkdKe', cmp_w2[l], eye).reshape(2, KV_WIDTH, KV_WIDTH).astype(BF16)

    head = jnp.arange(LANES, dtype=jnp.int32)[:, None]
    col = jnp.arange(SSD_WIDTH, dtype=jnp.int32)[None, :]
    idx = jnp.arange(SSD_CHUNK, dtype=jnp.int32)
    return {
        "norm1_w": norm1_w[l].reshape(1, D_MODEL), "w_in": w_perm,
        "ssd_conv_w": ssd_conv_w[l], "ssd_conv_b": ssd_conv_b[l].reshape(1, SSD_XBC),
        "dt_bias": _lane_row(ssd_dt_bias[l], LANES), "a_log": _lane_row(ssd_a_log[l], LANES),
        "d_skip": jnp.repeat(ssd_d[l], HEAD_DIM).reshape(1, SSD_WIDTH),
        "ssd_norm_w": ssd_norm_w[l].reshape(1, SSD_WIDTH), "sc_conv_w": sc_conv_w[l],
        "tril": (idx[:, None] >= idx[None, :]).astype(F32),
        "expand": (col // HEAD_DIM == head).astype(BF16),
        "cmp_pe": pe, "cmp_w2": w2,
        "cmp_wr": jnp.concatenate([wfs[:, 0], wfs[:, 1]], axis=-1), "cmp_w2t": w2.transpose(0, 2, 1),
        "w_out": w_out[l].astype(BF16), "norm2_w": norm2_w[l].reshape(1, D_MODEL),
        "w_ff1": w_ff1[l].astype(BF16), "w_ff2": w_ff2[l].astype(BF16),
    }


def _front_pad_rows(a, rows):
    return jnp.pad(a, ((0, 0), (rows - a.shape[1], 0), (0, 0)))


def _run_layer(h, lw, invf, ssm0, cssd0, csc0, attn_fn, final_norm_w, *, batch, seq_len, pos0, final,
               layer, depth, carried):
    (z, xbc, sc, qraw, qrope, kcmp, vcmp, ksel, vsel, kwin, vwin, small, kcmp_rows, vcmp_rows,
     kselb, vselb, kwinb, vwinb) = _inproj(h, lw["norm1_w"], lw["w_in"], invf, batch=batch, seq_len=seq_len, pos0=pos0,
                                           layer=layer, depth=depth, carried=carried)
    yssd, ysc, ssm_new, cssd_new, csc_new = _mixer(
        z, xbc, sc, small, ssm0, _front_pad_rows(cssd0, SUBLANES), _front_pad_rows(csc0, SUBLANES), lw,
        batch=batch, seq_len=seq_len)
    rows = dict(kcmp=kcmp, vcmp=vcmp, ksel=ksel, vsel=vsel, kwin=kwin, vwin=vwin, kcmp_rows=kcmp_rows,
                vcmp_rows=vcmp_rows, kselb=kselb, vselb=vselb, kwinb=kwinb, vwinb=vwinb)
    yatt = attn_fn(qrope, qraw, small, rows)
    h = _ffn(h, yssd, ysc, yatt, lw, final_norm_w, final=final)
    states = (ssm_new, cssd_new[:, SUBLANES - (SSD_CONV - 1):], csc_new[:, SUBLANES - (SC_CONV - 1):])
    return h, states, rows


def kernel(x_prompt, x_sample, cache_k_cmp, cache_v_cmp, cache_k_sel, cache_v_sel, cache_k_win, cache_v_win, state_ssm, state_ssd_conv, state_sc_conv, page_table, norm1_w, w_in, ssd_conv_w, ssd_conv_b, ssd_dt_bias, ssd_a_log, ssd_d, ssd_norm_w, sc_conv_w, cmp_pe, cmp_w1, cmp_w2, w_out, norm2_w, w_ff1, w_ff2, final_norm_w):
    bp, lp, _ = x_prompt.shape
    db, ls, _ = x_sample.shape
    depth = w_in.shape[0]
    n_pages = page_table.shape[1]
    past = n_pages * PAGE_SIZE
    win_buf = cache_k_win.shape[2]
    fnw = final_norm_w.reshape(1, D_MODEL)

    half = ROT_DIM // 2
    inv_freq = ROPE_THETA ** (-jnp.arange(half, dtype=F32) * 2.0 / ROT_DIM)
    invf = jnp.tile(inv_freq, LANES // half).reshape(1, LANES)

    hp = x_prompt.reshape(bp * lp, D_MODEL)
    hs = x_sample.reshape(db * ls, D_MODEL)
    ssm0_p = jnp.zeros((bp, SSD_HEADS, HEAD_DIM, SSD_STATE), F32)
    cssd0_p = jnp.zeros((bp, SSD_CONV - 1, SSD_XBC), F32)
    csc0_p = jnp.zeros((bp, SC_CONV - 1, SC_WIDTH), F32)
    kv_names = ("kcmp", "vcmp", "ksel", "vsel", "kwin", "vwin")
    p_kv_layers = []
    p_state_lists = [[] for _ in range(3)]
    s_lists = [[] for _ in range(9)]
    keep_p = min(WINDOW, lp)
    keep_s = min(WINDOW, win_buf + ls)
    prompt_dims_major = lp % min(INPROJ_ROWS, bp * lp) == 0
    carried_p = [jnp.zeros((depth, bp, KV_WIDTH, lp), F32) for _ in kv_names] if prompt_dims_major else []

    def kv4(a, b, n):
        return a.reshape(b, n, ATT_KV_HEADS, HEAD_DIM)

    def dims_major(c):
        return c.transpose(0, 1, 3, 4, 2).reshape(c.shape[0], c.shape[1], KV_WIDTH, c.shape[2])

    pages_kc, pages_vc, pages_ks, pages_vs = (dims_major(c) for c in (cache_k_cmp, cache_v_cmp, cache_k_sel, cache_v_sel))
    win_k, win_v = dims_major(cache_k_win), dims_major(cache_v_win)

    for l in range(depth):
        lw = _prep_layer(l, norm1_w, w_in, ssd_conv_w, ssd_conv_b, ssd_dt_bias, ssd_a_log, ssd_d, ssd_norm_w,
                         sc_conv_w, cmp_pe, cmp_w1, cmp_w2, w_out, norm2_w, w_ff1, w_ff2)
        final = l == depth - 1

        def prompt_attn(qrope, qraw, small, rows, lw=lw):
            ck, cvt = _compress_prompt(rows["kcmp_rows"], rows["vcmp_rows"], lw, batch=bp, seq_len=lp)
            return _attn_prompt(qrope, qraw, small, ck, cvt, rows["kselb"], rows["vselb"], rows["kwinb"], rows["vwinb"],
                                batch=bp, seq_len=lp)

        def sample_attn(qrope, qraw, small, rows, lw=lw, l=l):
            ck, cv = _compress_paged(page_table, pages_kc, pages_vc, lw, layer=l)
            return _attn_sample(page_table, pages_ks, pages_vs, win_k, win_v, qrope, qraw, small, ck, cv,
                                rows["kselb"], rows["vselb"], rows["kwinb"], rows["vwinb"], layer=l, seq_len=ls)

        hp, st_p, rows_p = _run_layer(hp, lw, invf, ssm0_p, cssd0_p, csc0_p, prompt_attn, fnw,
                                      batch=bp, seq_len=lp, pos0=0, final=final, layer=l, depth=depth, carried=carried_p)
        hs, st_s, rows_s = _run_layer(hs, lw, invf, state_ssm[l], state_ssd_conv[l], state_sc_conv[l], sample_attn, fnw,
                                      batch=db, seq_len=ls, pos0=past, final=final, layer=l, depth=depth, carried=[])
        if prompt_dims_major:
            carried_p = [rows_p[n] for n in kv_names]
        else:
            p_kv_layers.append([kv4(rows_p[n], bp, lp) for n in kv_names])
        for lst, arr in zip(p_state_lists, st_p):
            lst.append(arr)
        s_new = [kv4(rows_s[n], db, ls) for n in ("kcmp", "vcmp", "ksel", "vsel")]
        s_new += [jnp.concatenate([cache_k_win[l], kv4(rows_s["kwin"], db, ls)], axis=1)[:, -keep_s:],
                  jnp.concatenate([cache_v_win[l], kv4(rows_s["vwin"], db, ls)], axis=1)[:, -keep_s:]]
        s_new += list(st_s)
        for lst, arr in zip(s_lists, s_new):
            lst.append(arr)

    if prompt_dims_major:
        p_kv = [a.reshape(depth, bp, ATT_KV_HEADS, HEAD_DIM, lp).transpose(0, 1, 4, 2, 3) for a in carried_p]
    else:
        p_kv = [jnp.stack([lay[n] for lay in p_kv_layers]) for n in range(len(kv_names))]
    p_kv[4], p_kv[5] = p_kv[4][:, :, -keep_p:], p_kv[5][:, :, -keep_p:]
    p_out = p_kv + [jnp.stack(a) for a in p_state_lists]
    s_out = [jnp.stack(a) for a in s_lists]
    y_prompt = hp.reshape(bp, lp, D_MODEL)
    y_sample = hs.reshape(db, ls, D_MODEL)
    return (y_prompt, y_sample, *p_out, *s_out)
```
